```python
import math
import jax, jax.numpy as jnp
from jax import lax
import numpy as np

D_MODEL = 1024
BATCH = 2
SEQ = 8192
DEPTH = 2

CHUNK = 64
CONV_WIDTH = D_MODEL // 2
CONV_KERNEL = 31
ATT_HEADS = 4
ATT_HEAD_DIM = 64
ATT_VALUE_DIM = 2 * ATT_HEAD_DIM
ATT_WIDTH = ATT_HEADS * ATT_VALUE_DIM
MIX_WIDTH = CONV_WIDTH + ATT_WIDTH
QK_WIDTH = 2 * ATT_HEADS * ATT_HEAD_DIM
MIX_IN = 2 * CONV_WIDTH + 2 * QK_WIDTH + ATT_WIDTH
OFF_Q = 2 * CONV_WIDTH
OFF_K = OFF_Q + QK_WIDTH
OFF_V = OFF_K + QK_WIDTH
ROPE_THETA = 10000.0
Q_BLOCK = 128
N_GROUPS = 4
EXPERTS_PER_GROUP = 8
N_EXPERTS = N_GROUPS * EXPERTS_PER_GROUP
TOP_K_FINE = 2
D_EXPERT = D_MODEL // 2
MOE_BLOCK = 128
LN_EPS = 1e-5
DEEPNORM_ALPHA = (2.0 * DEPTH) ** 0.25
DEEPNORM_BETA = (8.0 * DEPTH) ** -0.25

kernel_name = "hymba_conformer_diffattn_hiermoe_deepnorm"


def _layernorm(x, g, b):
    xf = x.astype(jnp.float32)
    mu = jnp.mean(xf, axis=-1, keepdims=True)
    var = jnp.mean(jnp.square(xf - mu), axis=-1, keepdims=True)
    return ((xf - mu) * lax.rsqrt(var + LN_EPS) * g.astype(jnp.float32) + b.astype(jnp.float32)).astype(x.dtype)


def _rmsnorm(x, g):
    xf = x.astype(jnp.float32)
    ms = jnp.mean(jnp.square(xf), axis=-1, keepdims=True)
    return (xf * lax.rsqrt(ms + LN_EPS) * g.astype(jnp.float32)).astype(x.dtype)


def _rope_tables(seq):
    half = ATT_HEAD_DIM // 2
    inv_freq = 1.0 / (ROPE_THETA ** (jnp.arange(half, dtype=jnp.float32) * 2.0 / ATT_HEAD_DIM))
    ang = jnp.arange(seq, dtype=jnp.float32)[:, None] * inv_freq[None, :]
    return jnp.cos(ang)[:, None, :], jnp.sin(ang)[:, None, :]


def _rope(x, cos, sin):
    half = ATT_HEAD_DIM // 2
    xf = x.astype(jnp.float32)
    x1, x2 = xf[..., :half], xf[..., half:]
    return jnp.concatenate([x1 * cos - x2 * sin, x2 * cos + x1 * sin], axis=-1).astype(x.dtype)


def _causal_depthwise_conv(u, w, b):
    y = lax.conv_general_dilated(u, w.astype(u.dtype), window_strides=(1,),
                                 padding=[(CONV_KERNEL - 1, 0)],
                                 dimension_numbers=("NWC", "WIO", "NWC"),
                                 feature_group_count=u.shape[-1])
    return y + b


def _diff_attention(q, k, v, lam):
    bsz, seq = q.shape[0], q.shape[1]
    n_blocks = seq // Q_BLOCK
    scale = ATT_HEAD_DIM ** -0.5
    kf = k.astype(jnp.float32)
    vf = v.astype(jnp.float32)
    k_chunk = jnp.arange(seq) // CHUNK
    qb_all = jnp.swapaxes(q.reshape(bsz, n_blocks, Q_BLOCK, 2 * ATT_HEADS, ATT_HEAD_DIM), 0, 1)

    def one_block(args):
        qb, bi = args
        s = jnp.einsum("bqhd,bkhd->bhqk", qb.astype(jnp.float32), kf) * scale
        q_chunk = (bi * Q_BLOCK + jnp.arange(Q_BLOCK)) // CHUNK
        allowed = k_chunk[None, :] <= q_chunk[:, None]
        p = jax.nn.softmax(jnp.where(allowed[None, None], s, -jnp.inf), axis=-1)
        p = p.reshape(bsz, ATT_HEADS, 2, Q_BLOCK, seq)
        a = p[:, :, 0] - lam * p[:, :, 1]
        return jnp.einsum("bhqk,bkhe->bqhe", a, vf).astype(v.dtype)

    o = lax.map(one_block, (qb_all, jnp.arange(n_blocks)))
    return jnp.swapaxes(o, 0, 1).reshape(bsz, seq, ATT_HEADS, ATT_VALUE_DIM)


def _mixer(x, w_in, b_in, conv_w, conv_b, conv_ln_g, conv_ln_b,
           lam_q1, lam_k1, lam_q2, lam_k2, subln_g, w_out, cos, sin, lam_init):
    bsz, seq, _ = x.shape
    proj = x @ w_in + b_in
    u = proj[..., :CONV_WIDTH] * jax.nn.sigmoid(proj[..., CONV_WIDTH:OFF_Q])
    u = _causal_depthwise_conv(u, conv_w, conv_b)
    u = jax.nn.silu(_layernorm(u, conv_ln_g, conv_ln_b))
    q = _rope(proj[..., OFF_Q:OFF_K].reshape(bsz, seq, 2 * ATT_HEADS, ATT_HEAD_DIM), cos, sin)
    k = _rope(proj[..., OFF_K:OFF_V].reshape(bsz, seq, 2 * ATT_HEADS, ATT_HEAD_DIM), cos, sin)
    v = proj[..., OFF_V:].reshape(bsz, seq, ATT_HEADS, ATT_VALUE_DIM)
    lam = (jnp.exp(jnp.sum(lam_q1.astype(jnp.float32) * lam_k1.astype(jnp.float32)))
           - jnp.exp(jnp.sum(lam_q2.astype(jnp.float32) * lam_k2.astype(jnp.float32))) + lam_init)
    o = _diff_attention(q, k, v, lam)
    o = _rmsnorm(o, subln_g) * (1.0 - lam_init)
    merged = jnp.concatenate([u, o.reshape(bsz, seq, ATT_WIDTH)], axis=-1)
    return merged @ w_out


def _hier_moe(x, w_rg, b_rg, w_re, b_re, w_gate_e, w_up_e, w_down_e):
    bsz, seq, d = x.shape
    n_tok = bsz * seq
    xt = x.reshape(n_tok, d)
    pg = jax.nn.softmax((xt @ w_rg + b_rg).astype(jnp.float32), axis=-1)
    grp = jnp.argmax(pg, axis=-1)
    gate_g = jnp.max(pg, axis=-1)
    el = (xt @ w_re + b_re).astype(jnp.float32).reshape(n_tok, N_GROUPS, EXPERTS_PER_GROUP)
    el_g = jnp.take_along_axis(el, grp[:, None, None], axis=1)[:, 0]
    top_v, top_i = lax.top_k(el_g, TOP_K_FINE)
    wts = gate_g[:, None] * jax.nn.softmax(top_v, axis=-1)
    n_assign = n_tok * TOP_K_FINE
    eid = (grp[:, None] * EXPERTS_PER_GROUP + top_i).reshape(n_assign).astype(jnp.int32)
    tok = jnp.repeat(jnp.arange(n_tok, dtype=jnp.int32), TOP_K_FINE)
    w_a = wts.reshape(n_assign)
    order = jnp.argsort(eid)
    eid_s, tok_s, w_s = eid[order], tok[order], w_a[order]
    counts = jnp.zeros((N_EXPERTS,), jnp.int32).at[eid].add(1)
    padded = ((counts + MOE_BLOCK - 1) // MOE_BLOCK) * MOE_BLOCK
    pad_end = jnp.cumsum(padded)
    pad_start = pad_end - padded
    start = jnp.cumsum(counts) - counts
    dest = pad_start[eid_s] + (jnp.arange(n_assign, dtype=jnp.int32) - start[eid_s])
    n_blocks = -(-n_assign // MOE_BLOCK) + N_EXPERTS
    n_rows = n_blocks * MOE_BLOCK
    tok_pad = jnp.zeros((n_rows,), jnp.int32).at[dest].set(tok_s)
    w_pad = jnp.zeros((n_rows,), w_s.dtype).at[dest].set(w_s)
    blk_e = jnp.minimum(jnp.searchsorted(pad_end, jnp.arange(n_blocks, dtype=jnp.int32) * MOE_BLOCK,
                                         side="right"), N_EXPERTS - 1).astype(jnp.int32)

    def expert_block(args):
        tb, e = args
        xb = xt[tb]
        h = jax.nn.silu(xb @ w_gate_e[e]) * (xb @ w_up_e[e])
        return h @ w_down_e[e]

    yb = lax.map(expert_block, (tok_pad.reshape(n_blocks, MOE_BLOCK), blk_e))
    y = yb.reshape(n_rows, d) * w_pad[:, None].astype(x.dtype)
    out = jnp.zeros((n_tok, d), x.dtype).at[tok_pad].add(y)
    return out.reshape(bsz, seq, d)


def setup_inputs(seed: int = 0) -> dict:
    key = jax.random.key(seed)
    ks = jax.random.split(key, 24)
    nrm = jax.random.normal
    f32 = jnp.float32
    x = nrm(ks[0], (BATCH, SEQ, D_MODEL), f32)
    s_in = D_MODEL ** -0.5
    w_in = jnp.concatenate([
        nrm(ks[1], (DEPTH, D_MODEL, OFF_V), f32) * s_in,
        nrm(ks[2], (DEPTH, D_MODEL, ATT_WIDTH), f32) * s_in * DEEPNORM_BETA,
    ], axis=-1)
    b_in = 0.02 * nrm(ks[3], (DEPTH, MIX_IN), f32)
    conv_w = nrm(ks[4], (DEPTH, CONV_KERNEL, 1, CONV_WIDTH), f32) * CONV_KERNEL ** -0.5
    conv_b = 0.02 * nrm(ks[5], (DEPTH, CONV_WIDTH), f32)
    conv_ln_g = 1.0 + 0.05 * nrm(ks[6], (DEPTH, CONV_WIDTH), f32)
    conv_ln_b = 0.02 * nrm(ks[7], (DEPTH, CONV_WIDTH), f32)
    lam_q1 = 0.1 * nrm(ks[8], (DEPTH, ATT_HEAD_DIM), f32)
    lam_k1 = 0.1 * nrm(ks[9], (DEPTH, ATT_HEAD_DIM), f32)
    lam_q2 = 0.1 * nrm(ks[10], (DEPTH, ATT_HEAD_DIM), f32)
    lam_k2 = 0.1 * nrm(ks[11], (DEPTH, ATT_HEAD_DIM), f32)
    subln_g = 1.0 + 0.05 * nrm(ks[12], (DEPTH, ATT_VALUE_DIM), f32)
    w_out = nrm(ks[13], (DEPTH, MIX_WIDTH, D_MODEL), f32) * MIX_WIDTH ** -0.5 * DEEPNORM_BETA
    ln1_g = 1.0 + 0.05 * nrm(ks[14], (DEPTH, D_MODEL), f32)
    ln1_b = 0.02 * nrm(ks[15], (DEPTH, D_MODEL), f32)
    w_rg = nrm(ks[16], (DEPTH, D_MODEL, N_GROUPS), f32) * s_in
    b_rg = 0.01 * nrm(ks[17], (DEPTH, N_GROUPS), f32)
    w_re = nrm(ks[18], (DEPTH, D_MODEL, N_EXPERTS), f32) * s_in
    b_re = 0.01 * nrm(ks[19], (DEPTH, N_EXPERTS), f32)
    w_gate_e = nrm(ks[20], (DEPTH, N_EXPERTS, D_MODEL, D_EXPERT), f32) * s_in
    w_up_e = nrm(ks[21], (DEPTH, N_EXPERTS, D_MODEL, D_EXPERT), f32) * s_in
    w_down_e = nrm(ks[22], (DEPTH, N_EXPERTS, D_EXPERT, D_MODEL), f32) * D_EXPERT ** -0.5 * DEEPNORM_BETA
    k2a, k2b = jax.random.split(ks[23])
    ln2_g = 1.0 + 0.05 * nrm(k2a, (DEPTH, D_MODEL), f32)
    ln2_b = 0.02 * nrm(k2b, (DEPTH, D_MODEL), f32)
    return {"x": x, "w_in": w_in, "b_in": b_in, "conv_w": conv_w, "conv_b": conv_b,
            "conv_ln_g": conv_ln_g, "conv_ln_b": conv_ln_b,
            "lam_q1": lam_q1, "lam_k1": lam_k1, "lam_q2": lam_q2, "lam_k2": lam_k2,
            "subln_g": subln_g, "w_out": w_out, "ln1_g": ln1_g, "ln1_b": ln1_b,
            "w_rg": w_rg, "b_rg": b_rg, "w_re": w_re, "b_re": b_re,
            "w_gate_e": w_gate_e, "w_up_e": w_up_e, "w_down_e": w_down_e,
            "ln2_g": ln2_g, "ln2_b": ln2_b}


def reference(x, w_in, b_in, conv_w, conv_b, conv_ln_g, conv_ln_b,
              lam_q1, lam_k1, lam_q2, lam_k2, subln_g, w_out, ln1_g, ln1_b,
              w_rg, b_rg, w_re, b_re, w_gate_e, w_up_e, w_down_e, ln2_g, ln2_b):
    cos, sin = _rope_tables(x.shape[1])
    for l in range(DEPTH):
        lam_init = 0.8 - 0.6 * math.exp(-0.3 * l)
        m = _mixer(x, w_in[l], b_in[l], conv_w[l], conv_b[l], conv_ln_g[l], conv_ln_b[l],
                   lam_q1[l], lam_k1[l], lam_q2[l], lam_k2[l], subln_g[l], w_out[l],
                   cos, sin, lam_init)
        x = _layernorm(DEEPNORM_ALPHA * x + m, ln1_g[l], ln1_b[l])
        f = _hier_moe(x, w_rg[l], b_rg[l], w_re[l], b_re[l], w_gate_e[l], w_up_e[l], w_down_e[l])
        x = _layernorm(DEEPNORM_ALPHA * x + f, ln2_g[l], ln2_b[l])
    return x
```

```python
import functools
import math

import jax
import jax.numpy as jnp
from jax import lax
from jax.experimental import pallas as pl
from jax.experimental.pallas import tpu as pltpu

F32 = jnp.float32
BF16 = jnp.bfloat16

CHUNK = 64
CONV_KERNEL = 31
HEADS = 4
HEAD_DIM = 64
VALUE_DIM = 2 * HEAD_DIM
ROPE_THETA = 10000.0
N_GROUPS = 4
EXPERTS_PER_GROUP = 8
N_EXPERTS = N_GROUPS * EXPERTS_PER_GROUP
MOE_BLOCK = 128
LN_EPS = 1e-5
LANES = 128
CONV_HALO = 32
VMEM_LIMIT = 56 * 1024 * 1024


def _layernorm_rows(y, g, b):
    mu = jnp.mean(y, axis=-1, keepdims=True)
    d = y - mu
    var = jnp.mean(d * d, axis=-1, keepdims=True)
    return d * lax.rsqrt(var + LN_EPS) * g + b


def _inproj_kernel(x_ref, w_ref, b_ref, cos_ref, sin_ref, u_ref, q_ref, k_ref, v_ref, *, cw, qk):
    xb = x_ref[...].astype(BF16)

    def proj(c0, n):
        return jnp.dot(xb, w_ref[:, c0:c0 + n], preferred_element_type=F32) + b_ref[:, c0:c0 + n]

    ag = proj(0, 2 * cw)
    u_ref[...] = ag[:, :cw] * jax.nn.sigmoid(ag[:, cw:])

    cos = cos_ref[...]
    sin = sin_ref[...]
    lane = lax.broadcasted_iota(jnp.int32, cos.shape, 1)
    first_half = (lane % HEAD_DIM) < (HEAD_DIM // 2)

    def rope(z, scale):
        outs = []
        for c in range(0, qk, LANES):
            zz = z[:, c:c + LANES]
            swapped = jnp.where(first_half, pltpu.roll(zz, LANES - HEAD_DIM // 2, 1),
                                pltpu.roll(zz, HEAD_DIM // 2, 1))
            outs.append((zz * cos + swapped * sin) * scale)
        return jnp.concatenate(outs, axis=1)

    q_ref[...] = rope(proj(2 * cw, qk), HEAD_DIM ** -0.5).astype(BF16)
    k_ref[...] = rope(proj(2 * cw + qk, qk), 1.0).astype(BF16)
    v_ref[...] = proj(2 * cw + 2 * qk, v_ref.shape[1]).astype(BF16)


def _inproj(x2, w_bf, b, cos_t, sin_t, *, seq, tm):
    t, d = x2.shape
    n = w_bf.shape[1]
    cw = d // 2
    qk = 2 * HEADS * HEAD_DIM
    vw = HEADS * VALUE_DIM
    tiles_per_seq = seq // tm
    return pl.pallas_call(
        functools.partial(_inproj_kernel, cw=cw, qk=qk),
        grid=(t // tm,),
        in_specs=[
            pl.BlockSpec((tm, d), lambda i: (i, 0)),
            pl.BlockSpec((d, n), lambda i: (0, 0)),
            pl.BlockSpec((1, n), lambda i: (0, 0)),
            pl.BlockSpec((tm, LANES), lambda i: (i % tiles_per_seq, 0)),
            pl.BlockSpec((tm, LANES), lambda i: (i % tiles_per_seq, 0)),
        ],
        out_specs=[
            pl.BlockSpec((tm, cw), lambda i: (i, 0)),
            pl.BlockSpec((tm, qk), lambda i: (i, 0)),
            pl.BlockSpec((tm, qk), lambda i: (i, 0)),
            pl.BlockSpec((tm, vw), lambda i: (i, 0)),
        ],
        out_shape=[
            jax.ShapeDtypeStruct((t, cw), F32),
            jax.ShapeDtypeStruct((t, qk), BF16),
            jax.ShapeDtypeStruct((t, qk), BF16),
            jax.ShapeDtypeStruct((t, vw), BF16),
        ],
        compiler_params=pltpu.CompilerParams(dimension_semantics=("arbitrary",), vmem_limit_bytes=VMEM_LIMIT),
        name="inproj",
    )(x2, w_bf, b, cos_t, sin_t)


def _conv_kernel(u_ref, w_ref, b_ref, g_ref, beta_ref, o_ref, buf_ref, *, ts, tiles_per_seq, rb):
    i = pl.program_id(0)

    @pl.when(i % tiles_per_seq == 0)
    def _():
        buf_ref[0:CONV_HALO, :] = jnp.zeros((CONV_HALO, buf_ref.shape[1]), F32)

    @pl.when(i % tiles_per_seq != 0)
    def _():
        buf_ref[0:CONV_HALO, :] = buf_ref[ts:ts + CONV_HALO, :]

    buf_ref[CONV_HALO:CONV_HALO + ts, :] = u_ref[...]

    bias = b_ref[...]
    g = g_ref[...]
    beta = beta_ref[...]
    first = CONV_HALO - (CONV_KERNEL - 1)
    for r0 in range(0, ts, rb):
        acc = jnp.broadcast_to(bias, (rb, bias.shape[1]))
        for j in range(CONV_KERNEL):
            acc = acc + w_ref[j:j + 1, :] * buf_ref[r0 + first + j:r0 + first + j + rb, :]
        y = _layernorm_rows(acc, g, beta)
        o_ref[r0:r0 + rb, :] = (y * jax.nn.sigmoid(y)).astype(o_ref.dtype)


def _conv_module(u, w, b, g, beta, *, seq, ts, rb=32):
    t, c = u.shape
    tiles_per_seq = seq // ts
    return pl.pallas_call(
        functools.partial(_conv_kernel, ts=ts, tiles_per_seq=tiles_per_seq, rb=rb),
        grid=(t // ts,),
        in_specs=[
            pl.BlockSpec((ts, c), lambda i: (i, 0)),
            pl.BlockSpec((CONV_KERNEL, c), lambda i: (0, 0)),
            pl.BlockSpec((1, c), lambda i: (0, 0)),
            pl.BlockSpec((1, c), lambda i: (0, 0)),
            pl.BlockSpec((1, c), lambda i: (0, 0)),
        ],
        out_specs=pl.BlockSpec((ts, c), lambda i: (i, 0)),
        out_shape=jax.ShapeDtypeStruct((t, c), BF16),
        scratch_shapes=[pltpu.VMEM((CONV_HALO + ts, c), F32)],
        compiler_params=pltpu.CompilerParams(dimension_semantics=("arbitrary",)),
        name="conv_module",
    )(u, w, b, g, beta)


def _attn_kernel(lam_ref, g_ref, q_ref, k_ref, v_ref, o_ref, m_scr, l_scr, acc_scr, *, tq, lam_init):
    qi = pl.program_id(2)
    q = q_ref[...]
    lane = lax.broadcasted_iota(jnp.int32, q.shape, 1)
    zero = jnp.zeros_like(q)
    qq = jnp.concatenate([jnp.where(lane < HEAD_DIM, q, zero), jnp.where(lane >= HEAD_DIM, q, zero)], axis=0)

    m_scr[...] = jnp.full(m_scr.shape, -jnp.inf, F32)
    l_scr[...] = jnp.zeros(l_scr.shape, F32)
    acc_scr[...] = jnp.zeros(acc_scr.shape, F32)

    def step(j, masked):
        start = pl.multiple_of(j * tq, tq)
        k = k_ref[pl.ds(start, tq), :]
        v = v_ref[pl.ds(start, tq), :]
        s = lax.dot_general(qq, k, (((1,), (1,)), ((), ())), preferred_element_type=F32)
        if masked:
            row = lax.broadcasted_iota(jnp.int32, s.shape, 0) % tq
            col = lax.broadcasted_iota(jnp.int32, s.shape, 1)
            s = jnp.where((col // CHUNK) <= (row // CHUNK), s, -jnp.inf)
        m_prev = m_scr[...]
        m_new = jnp.maximum(m_prev, jnp.max(s, axis=-1, keepdims=True))
        alpha = jnp.exp(m_prev - m_new)
        p = jnp.exp(s - jnp.tile(m_new, (1, tq // LANES)))
        l_scr[...] = alpha * l_scr[...] + jnp.sum(p, axis=-1, keepdims=True)
        acc_scr[...] = alpha * acc_scr[...] + jnp.dot(p.astype(BF16), v, preferred_element_type=F32)
        m_scr[...] = m_new

    def body(j, carry):
        step(j, False)
        return carry

    lax.fori_loop(0, qi, body, 0)
    step(qi, True)

    lp = lam_ref[...]
    lam = (jnp.exp(jnp.sum(lp[0:1] * lp[1:2], axis=-1, keepdims=True))
           - jnp.exp(jnp.sum(lp[2:3] * lp[3:4], axis=-1, keepdims=True)) + lam_init)
    o = acc_scr[...] / l_scr[...]
    o = o[:tq] - lam * o[tq:]
    ms = jnp.mean(o * o, axis=-1, keepdims=True)
    o_ref[...] = (o * lax.rsqrt(ms + LN_EPS) * g_ref[...] * (1.0 - lam_init)).astype(o_ref.dtype)


def _diff_attention(q, k, v, lam_params, subln_g, *, batch, seq, tq, lam_init):
    t = q.shape[0]
    nq = seq // tq
    return pl.pallas_call(
        functools.partial(_attn_kernel, tq=tq, lam_init=lam_init),
        grid=(batch, HEADS, nq),
        in_specs=[
            pl.BlockSpec((4, HEAD_DIM), lambda b, h, i: (0, 0)),
            pl.BlockSpec((1, VALUE_DIM), lambda b, h, i: (0, 0)),
            pl.BlockSpec((tq, VALUE_DIM), lambda b, h, i: (b * nq + i, h)),
            pl.BlockSpec((seq, VALUE_DIM), lambda b, h, i: (b, h)),
            pl.BlockSpec((seq, VALUE_DIM), lambda b, h, i: (b, h)),
        ],
        out_specs=pl.BlockSpec((tq, VALUE_DIM), lambda b, h, i: (b * nq + i, h)),
        out_shape=jax.ShapeDtypeStruct((t, HEADS * VALUE_DIM), BF16),
        scratch_shapes=[
            pltpu.VMEM((2 * tq, LANES), F32),
            pltpu.VMEM((2 * tq, LANES), F32),
            pltpu.VMEM((2 * tq, VALUE_DIM), F32),
        ],
        compiler_params=pltpu.CompilerParams(dimension_semantics=("arbitrary", "arbitrary", "arbitrary")),
        name="diff_attention",
    )(lam_params, subln_g, q, k, v)


def _outproj_kernel(x_ref, u_ref, o_ref, w_ref, g_ref, b_ref, wr_ref, br_ref,
                    x1_ref, ri_ref, rw_ref, *, alpha, cw):
    acc = jnp.dot(u_ref[...], w_ref[0:cw, :], preferred_element_type=F32)
    acc = acc + jnp.dot(o_ref[...], w_ref[cw:, :], preferred_element_type=F32)
    x1 = _layernorm_rows(alpha * x_ref[...] + acc, g_ref[...], b_ref[...])
    x1_ref[...] = x1

    logits = jnp.dot(x1, wr_ref[...], preferred_element_type=F32, precision=lax.Precision.HIGHEST) + br_ref[...]
    lane = lax.broadcasted_iota(jnp.int32, logits.shape, 1)
    neg = -jnp.inf
    gmask = lane < N_GROUPS
    lg = jnp.where(gmask, logits, neg)
    gmax = jnp.max(lg, axis=-1, keepdims=True)
    grp = jnp.min(jnp.where(lg == gmax, lane, LANES), axis=-1, keepdims=True)
    gsum = jnp.sum(jnp.where(gmask, jnp.exp(lg - gmax), 0.0), axis=-1, keepdims=True)
    gate = 1.0 / gsum
    emask = (lane >= N_GROUPS) & (lane < N_GROUPS + N_EXPERTS) & (((lane - N_GROUPS) // EXPERTS_PER_GROUP) == grp)
    ev = jnp.where(emask, logits, neg)
    v0 = jnp.max(ev, axis=-1, keepdims=True)
    i0 = jnp.min(jnp.where(emask & (ev == v0), lane, LANES), axis=-1, keepdims=True)
    emask1 = emask & (lane != i0)
    ev1 = jnp.where(emask1, logits, neg)
    v1 = jnp.max(ev1, axis=-1, keepdims=True)
    i1 = jnp.min(jnp.where(emask1 & (ev1 == v1), lane, LANES), axis=-1, keepdims=True)
    e1 = jnp.exp(v1 - v0)
    w0 = gate * (1.0 / (1.0 + e1))
    w1 = gate * (e1 / (1.0 + e1))
    ri_ref[...] = jnp.where(lane == 0, i0 - N_GROUPS, jnp.where(lane == 1, i1 - N_GROUPS, 0))
    rw_ref[...] = jnp.where(lane == 0, w0, jnp.where(lane == 1, w1, 0.0))


def _outproj_router(x2, u2, o, w_bf, g, b, wr, br, *, tm, alpha):
    t, d = x2.shape
    cw = u2.shape[1]
    row = lambda i: (i, 0)
    const = lambda i: (0, 0)
    return pl.pallas_call(
        functools.partial(_outproj_kernel, alpha=alpha, cw=cw),
        grid=(t // tm,),
        in_specs=[
            pl.BlockSpec((tm, d), row),
            pl.BlockSpec((tm, cw), row),
            pl.BlockSpec((tm, o.shape[1]), row),
            pl.BlockSpec(w_bf.shape, const),
            pl.BlockSpec((1, d), const),
            pl.BlockSpec((1, d), const),
            pl.BlockSpec((d, LANES), const),
            pl.BlockSpec((1, LANES), const),
        ],
        out_specs=[
            pl.BlockSpec((tm, d), row),
            pl.BlockSpec((tm, LANES), row),
            pl.BlockSpec((tm, LANES), row),
        ],
        out_shape=[
            jax.ShapeDtypeStruct((t, d), F32),
            jax.ShapeDtypeStruct((t, LANES), jnp.int32),
            jax.ShapeDtypeStruct((t, LANES), F32),
        ],
        compiler_params=pltpu.CompilerParams(dimension_semantics=("arbitrary",), vmem_limit_bytes=VMEM_LIMIT),
        name="outproj_router",
    )(x2, u2, o, w_bf, g, b, wr, br)


def _row_gather_copy(src_hbm, dst_buf, sem, slot, row, src_row):
    return pltpu.make_async_copy(src_hbm.at[pl.ds(src_row, 1)], dst_buf.at[slot, pl.ds(row, 1)], sem.at[slot])


def _expert_kernel(tok_ref, blk_e_ref, x_hbm, wp_ref, wg_ref, wu_ref, wd_ref, y_ref, xbuf, sem):
    del blk_e_ref
    b = pl.program_id(0)
    nb = pl.num_programs(0)
    slot = b % 2

    def issue(blk, s):
        for r in range(MOE_BLOCK):
            _row_gather_copy(x_hbm, xbuf, sem, s, r, tok_ref[blk * MOE_BLOCK + r]).start()

    @pl.when(b == 0)
    def _():
        issue(0, 0)

    @pl.when(b + 1 < nb)
    def _():
        issue(b + 1, 1 - slot)

    for r in range(MOE_BLOCK):
        _row_gather_copy(x_hbm, xbuf, sem, slot, r, 0).wait()

    xb = xbuf[slot].astype(BF16)
    gate = jnp.dot(xb, wg_ref[...], preferred_element_type=F32)
    up = jnp.dot(xb, wu_ref[...], preferred_element_type=F32)
    h = (gate * jax.nn.sigmoid(gate)) * up
    y = jnp.dot(h.astype(BF16), wd_ref[...], preferred_element_type=F32)
    y_ref[...] = y * wp_ref[...]


def _expert_mlp(tok_pad, blk_e, x1, w_pad, wg_bf, wu_bf, wd_bf):
    n_rows = tok_pad.shape[0]
    n_blocks = n_rows // MOE_BLOCK
    d = x1.shape[1]
    de = wg_bf.shape[2]
    grid_spec = pltpu.PrefetchScalarGridSpec(
        num_scalar_prefetch=2,
        grid=(n_blocks,),
        in_specs=[
            pl.BlockSpec(memory_space=pl.ANY),
            pl.BlockSpec((MOE_BLOCK, 1), lambda i, tok, be: (i, 0)),
            pl.BlockSpec((None, d, de), lambda i, tok, be: (be[i], 0, 0)),
            pl.BlockSpec((None, d, de), lambda i, tok, be: (be[i], 0, 0)),
            pl.BlockSpec((None, de, d), lambda i, tok, be: (be[i], 0, 0)),
        ],
        out_specs=pl.BlockSpec((MOE_BLOCK, d), lambda i, tok, be: (i, 0)),
        scratch_shapes=[pltpu.VMEM((2, MOE_BLOCK, d), F32), pltpu.SemaphoreType.DMA((2,))],
    )
    return pl.pallas_call(
        _expert_kernel,
        grid_spec=grid_spec,
        out_shape=jax.ShapeDtypeStruct((n_rows, d), F32),
        compiler_params=pltpu.CompilerParams(dimension_semantics=("arbitrary",), vmem_limit_bytes=VMEM_LIMIT),
        name="expert_mlp",
    )(tok_pad, blk_e, x1, w_pad, wg_bf, wu_bf, wd_bf)


def _combine_kernel(pos_ref, y_hbm, x_ref, g_ref, b_ref, o_ref, ybuf, sem, *, tm, alpha):
    i = pl.program_id(0)
    n = pl.num_programs(0)
    slot = i % 2

    def issue(tile, s):
        for r in range(2 * tm):
            _row_gather_copy(y_hbm, ybuf, sem, s, r, pos_ref[tile * 2 * tm + r]).start()

    @pl.when(i == 0)
    def _():
        issue(0, 0)

    @pl.when(i + 1 < n)
    def _():
        issue(i + 1, 1 - slot)

    for r in range(2 * tm):
        _row_gather_copy(y_hbm, ybuf, sem, slot, r, 0).wait()

    yy = ybuf[slot]
    f = yy[:tm] + yy[tm:]
    o_ref[...] = _layernorm_rows(alpha * x_ref[...] + f, g_ref[...], b_ref[...])


def _combine(pos, y, x1, g, b, *, tm, alpha):
    t, d = x1.shape
    grid_spec = pltpu.PrefetchScalarGridSpec(
        num_scalar_prefetch=1,
        grid=(t // tm,),
        in_specs=[
            pl.BlockSpec(memory_space=pl.ANY),
            pl.BlockSpec((tm, d), lambda i, pos: (i, 0)),
            pl.BlockSpec((1, d), lambda i, pos: (0, 0)),
            pl.BlockSpec((1, d), lambda i, pos: (0, 0)),
        ],
        out_specs=pl.BlockSpec((tm, d), lambda i, pos: (i, 0)),
        scratch_shapes=[pltpu.VMEM((2, 2 * tm, d), F32), pltpu.SemaphoreType.DMA((2,))],
    )
    return pl.pallas_call(
        functools.partial(_combine_kernel, tm=tm, alpha=alpha),
        grid_spec=grid_spec,
        out_shape=jax.ShapeDtypeStruct((t, d), F32),
        compiler_params=pltpu.CompilerParams(dimension_semantics=("arbitrary",), vmem_limit_bytes=VMEM_LIMIT),
        name="moe_combine",
    )(pos, y, x1, g, b)


def _dispatch_plan(route_i, route_w, *, tm):
    t = route_i.shape[0]
    eid = route_i[:, :2].reshape(-1)
    wts = route_w[:, :2].reshape(-1)
    n_assign = eid.shape[0]
    onehot = (eid[:, None] == jnp.arange(N_EXPERTS, dtype=jnp.int32)[None, :]).astype(jnp.int32)
    csum = jnp.cumsum(onehot, axis=0)
    rank = jnp.take_along_axis(csum, eid[:, None], axis=1)[:, 0] - 1
    counts = csum[-1]
    padded = ((counts + MOE_BLOCK - 1) // MOE_BLOCK) * MOE_BLOCK
    pad_end = jnp.cumsum(padded)
    pad_start = pad_end - padded
    dest = pad_start[eid] + rank
    n_blocks = -(-n_assign // MOE_BLOCK) + N_EXPERTS
    n_rows = n_blocks * MOE_BLOCK
    tok = jnp.arange(n_assign, dtype=jnp.int32) // 2
    tok_pad = jnp.zeros((n_rows,), jnp.int32).at[dest].set(tok)
    w_pad = jnp.zeros((n_rows,), F32).at[dest].set(wts)
    blk_e = jnp.minimum(jnp.searchsorted(pad_end, jnp.arange(n_blocks, dtype=jnp.int32) * MOE_BLOCK, side="right"),
                        N_EXPERTS - 1).astype(jnp.int32)
    pos = dest.reshape(t // tm, tm, 2).transpose(0, 2, 1).reshape(-1).astype(jnp.int32)
    return tok_pad, w_pad.reshape(n_rows, 1), blk_e, pos


def _rope_tables(seq):
    half = HEAD_DIM // 2
    inv_freq = 1.0 / (ROPE_THETA ** (jnp.arange(half, dtype=F32) * 2.0 / HEAD_DIM))
    ang = jnp.arange(seq, dtype=F32)[:, None] * inv_freq[None, :]
    cos, sin = jnp.cos(ang), jnp.sin(ang)
    reps = LANES // HEAD_DIM
    return jnp.tile(jnp.concatenate([cos, cos], axis=1), (1, reps)), jnp.tile(jnp.concatenate([-sin, sin], axis=1), (1, reps))


def _pick_tile(n, want):
    while n % want:
        want //= 2
    return want


def kernel(x, w_in, b_in, conv_w, conv_b, conv_ln_g, conv_ln_b, lam_q1, lam_k1, lam_q2, lam_k2, subln_g, w_out, ln1_g, ln1_b, w_rg, b_rg, w_re, b_re, w_gate_e, w_up_e, w_down_e, ln2_g, ln2_b):
    batch, seq, d = x.shape
    depth = w_in.shape[0]
    t = batch * seq
    alpha = (2.0 * depth) ** 0.25
    cos_t, sin_t = _rope_tables(seq)
    tm = _pick_tile(seq, 512)
    ts = _pick_tile(seq, 256)
    tq = _pick_tile(seq, 256)
    tc = _pick_tile(seq, 256)
    row = lambda a: a.reshape(1, -1)

    x2 = x.reshape(t, d)
    for l in range(depth):
        lam_init = 0.8 - 0.6 * math.exp(-0.3 * l)
        u, q, k, v = _inproj(x2, w_in[l].astype(BF16), row(b_in[l]), cos_t, sin_t, seq=seq, tm=tm)
        u2 = _conv_module(u, conv_w[l].reshape(CONV_KERNEL, -1), row(conv_b[l]), row(conv_ln_g[l]),
                          row(conv_ln_b[l]), seq=seq, ts=ts)
        lam_params = jnp.stack([lam_q1[l], lam_k1[l], lam_q2[l], lam_k2[l]])
        o = _diff_attention(q, k, v, lam_params, row(subln_g[l]), batch=batch, seq=seq, tq=tq, lam_init=lam_init)
        w_router = jnp.concatenate([w_rg[l], w_re[l]], axis=1)
        w_router = jnp.pad(w_router, ((0, 0), (0, LANES - w_router.shape[1])))
        b_router = jnp.pad(jnp.concatenate([b_rg[l], b_re[l]]), (0, LANES - N_GROUPS - N_EXPERTS))
        x1, route_i, route_w = _outproj_router(x2, u2, o, w_out[l].astype(BF16), row(ln1_g[l]), row(ln1_b[l]),
                                               w_router, row(b_router), tm=tm, alpha=alpha)
        tok_pad, w_pad, blk_e, pos = _dispatch_plan(route_i, route_w, tm=tc)
        y = _expert_mlp(tok_pad, blk_e, x1, w_pad, w_gate_e[l].astype(BF16), w_up_e[l].astype(BF16),
                        w_down_e[l].astype(BF16))
        x2 = _combine(pos, y, x1, row(ln2_g[l]), row(ln2_b[l]), tm=tc, alpha=alpha)
    return x2.reshape(batch, seq, d)
```

```python
import functools
import math

import jax
import jax.numpy as jnp
from jax import lax
from jax.experimental import pallas as pl
from jax.experimental.pallas import tpu as pltpu

F32 = jnp.float32
BF16 = jnp.bfloat16

CHUNK = 64
CONV_KERNEL = 31
HEADS = 4
HEAD_DIM = 64
VALUE_DIM = 2 * HEAD_DIM
ROPE_THETA = 10000.0
N_GROUPS = 4
EXPERTS_PER_GROUP = 8
N_EXPERTS = N_GROUPS * EXPERTS_PER_GROUP
MOE_BLOCK = 128
LN_EPS = 1e-5
LANES = 128
CONV_HALO = 32
VMEM_LIMIT = 56 * 1024 * 1024


def _layernorm_rows(y, g, b):
    mu = jnp.mean(y, axis=-1, keepdims=True)
    d = y - mu
    var = jnp.mean(d * d, axis=-1, keepdims=True)
    return d * lax.rsqrt(var + LN_EPS) * g + b


def _inproj_kernel(x_ref, w_ref, b_ref, cos_ref, sin_ref, u_ref, q_ref, k_ref, v_ref, *, cw, qk):
    xb = x_ref[...].astype(BF16)

    def proj(c0, n):
        return jnp.dot(xb, w_ref[:, c0:c0 + n], preferred_element_type=F32) + b_ref[:, c0:c0 + n]

    ag = proj(0, 2 * cw)
    u_ref[...] = ag[:, :cw] * jax.nn.sigmoid(ag[:, cw:])

    cos = cos_ref[...]
    sin = sin_ref[...]
    lane = lax.broadcasted_iota(jnp.int32, cos.shape, 1)
    first_half = (lane % HEAD_DIM) < (HEAD_DIM // 2)

    def rope(z, scale):
        outs = []
        for c in range(0, qk, LANES):
            zz = z[:, c:c + LANES]
            swapped = jnp.where(first_half, pltpu.roll(zz, LANES - HEAD_DIM // 2, 1),
                                pltpu.roll(zz, HEAD_DIM // 2, 1))
            outs.append((zz * cos + swapped * sin) * scale)
        return jnp.concatenate(outs, axis=1)

    q_ref[...] = rope(proj(2 * cw, qk), HEAD_DIM ** -0.5 * math.log2(math.e)).astype(BF16)
    k_ref[...] = rope(proj(2 * cw + qk, qk), 1.0).astype(BF16)
    v_ref[...] = proj(2 * cw + 2 * qk, v_ref.shape[1]).astype(BF16)


def _inproj(x2, w_bf, b, cos_t, sin_t, *, seq, tm):
    t, d = x2.shape
    n = w_bf.shape[1]
    cw = d // 2
    qk = 2 * HEADS * HEAD_DIM
    vw = HEADS * VALUE_DIM
    tiles_per_seq = seq // tm
    return pl.pallas_call(
        functools.partial(_inproj_kernel, cw=cw, qk=qk),
        grid=(t // tm,),
        in_specs=[
            pl.BlockSpec((tm, d), lambda i: (i, 0)),
            pl.BlockSpec((d, n), lambda i: (0, 0)),
            pl.BlockSpec((1, n), lambda i: (0, 0)),
            pl.BlockSpec((tm, LANES), lambda i: (i % tiles_per_seq, 0)),
            pl.BlockSpec((tm, LANES), lambda i: (i % tiles_per_seq, 0)),
        ],
        out_specs=[
            pl.BlockSpec((tm, cw), lambda i: (i, 0)),
            pl.BlockSpec((tm, qk), lambda i: (i, 0)),
            pl.BlockSpec((tm, qk), lambda i: (i, 0)),
            pl.BlockSpec((tm, vw), lambda i: (i, 0)),
        ],
        out_shape=[
            jax.ShapeDtypeStruct((t, cw), F32),
            jax.ShapeDtypeStruct((t, qk), BF16),
            jax.ShapeDtypeStruct((t, qk), BF16),
            jax.ShapeDtypeStruct((t, vw), BF16),
        ],
        compiler_params=pltpu.CompilerParams(dimension_semantics=("arbitrary",), vmem_limit_bytes=VMEM_LIMIT),
        name="inproj",
    )(x2, w_bf, b, cos_t, sin_t)


def _conv_kernel(u_ref, w_ref, b_ref, g_ref, beta_ref, o_ref, buf_ref, *, ts, tiles_per_seq, rb):
    i = pl.program_id(0)

    @pl.when(i % tiles_per_seq == 0)
    def _():
        buf_ref[0:CONV_HALO, :] = jnp.zeros((CONV_HALO, buf_ref.shape[1]), F32)

    @pl.when(i % tiles_per_seq != 0)
    def _():
        buf_ref[0:CONV_HALO, :] = buf_ref[ts:ts + CONV_HALO, :]

    buf_ref[CONV_HALO:CONV_HALO + ts, :] = u_ref[...]

    bias = b_ref[...]
    g = g_ref[...]
    beta = beta_ref[...]
    first = CONV_HALO - (CONV_KERNEL - 1)
    for r0 in range(0, ts, rb):
        acc = jnp.broadcast_to(bias, (rb, bias.shape[1]))
        for j in range(CONV_KERNEL):
            acc = acc + w_ref[j:j + 1, :] * buf_ref[r0 + first + j:r0 + first + j + rb, :]
        y = _layernorm_rows(acc, g, beta)
        o_ref[r0:r0 + rb, :] = (y * jax.nn.sigmoid(y)).astype(o_ref.dtype)


def _conv_module(u, w, b, g, beta, *, seq, ts, rb=32):
    t, c = u.shape
    tiles_per_seq = seq // ts
    return pl.pallas_call(
        functools.partial(_conv_kernel, ts=ts, tiles_per_seq=tiles_per_seq, rb=rb),
        grid=(t // ts,),
        in_specs=[
            pl.BlockSpec((ts, c), lambda i: (i, 0)),
            pl.BlockSpec((CONV_KERNEL, c), lambda i: (0, 0)),
            pl.BlockSpec((1, c), lambda i: (0, 0)),
            pl.BlockSpec((1, c), lambda i: (0, 0)),
            pl.BlockSpec((1, c), lambda i: (0, 0)),
        ],
        out_specs=pl.BlockSpec((ts, c), lambda i: (i, 0)),
        out_shape=jax.ShapeDtypeStruct((t, c), BF16),
        scratch_shapes=[pltpu.VMEM((CONV_HALO + ts, c), F32)],
        compiler_params=pltpu.CompilerParams(dimension_semantics=("arbitrary",)),
        name="conv_module",
    )(u, w, b, g, beta)


def _attn_kernel(lam_ref, g_ref, q_ref, k_ref, v_ref, o_ref, qq_scr, s_scr, p_scr, a_scr, m_scr, l_scr, acc_scr,
                 *, tq, rc, lam_init):
    qi = pl.program_id(2)
    q = q_ref[...]
    lane = lax.broadcasted_iota(jnp.int32, q.shape, 1)
    zero = jnp.zeros_like(q)
    qq_scr[0:tq, :] = jnp.where(lane < HEAD_DIM, q, zero)
    qq_scr[tq:, :] = jnp.where(lane >= HEAD_DIM, q, zero)

    m_scr[...] = jnp.full(m_scr.shape, -jnp.inf, F32)
    l_scr[...] = jnp.zeros(l_scr.shape, F32)
    acc_scr[...] = jnp.zeros(acc_scr.shape, F32)

    def kv_rows(j):
        return pl.ds(pl.multiple_of(j * tq, tq), tq)

    def scores(j, slot):
        s_scr[slot] = lax.dot_general(qq_scr[...], k_ref[kv_rows(j), :], (((1,), (1,)), ((), ())),
                                      preferred_element_type=F32)

    def softmax(slot, masked):
        for r0 in range(0, 2 * tq, rc):
            s = s_scr[slot, r0:r0 + rc, :]
            if masked:
                row = (lax.broadcasted_iota(jnp.int32, s.shape, 0) + r0) % tq
                col = lax.broadcasted_iota(jnp.int32, s.shape, 1)
                s = jnp.where((col // CHUNK) <= (row // CHUNK), s, -jnp.inf)
            m_prev = m_scr[r0:r0 + rc, :]
            m_new = jnp.maximum(m_prev, jnp.max(s, axis=-1, keepdims=True))
            alpha = jnp.exp2(m_prev - m_new)
            p = jnp.exp2(s - jnp.tile(m_new, (1, tq // LANES)))
            l_scr[r0:r0 + rc, :] = alpha * l_scr[r0:r0 + rc, :] + jnp.sum(p, axis=-1, keepdims=True)
            m_scr[r0:r0 + rc, :] = m_new
            a_scr[slot, r0:r0 + rc, :] = alpha
            p_scr[slot, r0:r0 + rc, :] = p.astype(BF16)

    def values(j, slot):
        acc_scr[...] = a_scr[slot] * acc_scr[...] + jnp.dot(p_scr[slot], v_ref[kv_rows(j), :],
                                                            preferred_element_type=F32)

    scores(0, 0)

    @pl.when(qi > 0)
    def _():
        scores(1, 1)
        softmax(0, False)

    def body(i, carry):
        for par in (0, 1):
            @pl.when(i % 2 == par)
            def _():
                scores(i + 1, 1 - par)
                softmax(par, False)
                values(i - 1, 1 - par)
        return carry

    lax.fori_loop(1, qi, body, 0)

    for par in (0, 1):
        @pl.when((qi > 0) & (qi % 2 == par))
        def _():
            softmax(par, True)
            values(qi - 1, 1 - par)
            values(qi, par)

    @pl.when(qi == 0)
    def _():
        softmax(0, True)
        values(0, 0)

    lp = lam_ref[...]
    lam = (jnp.exp(jnp.sum(lp[0:1] * lp[1:2], axis=-1, keepdims=True))
           - jnp.exp(jnp.sum(lp[2:3] * lp[3:4], axis=-1, keepdims=True)) + lam_init)
    o = acc_scr[...] / l_scr[...]
    o = o[:tq] - lam * o[tq:]
    ms = jnp.mean(o * o, axis=-1, keepdims=True)
    o_ref[...] = (o * lax.rsqrt(ms + LN_EPS) * g_ref[...] * (1.0 - lam_init)).astype(o_ref.dtype)


def _diff_attention(q, k, v, lam_params, subln_g, *, batch, seq, tq, lam_init):
    t = q.shape[0]
    nq = seq // tq
    return pl.pallas_call(
        functools.partial(_attn_kernel, tq=tq, rc=min(256, tq), lam_init=lam_init),
        grid=(batch, HEADS, nq),
        in_specs=[
            pl.BlockSpec((4, HEAD_DIM), lambda b, h, i: (0, 0)),
            pl.BlockSpec((1, VALUE_DIM), lambda b, h, i: (0, 0)),
            pl.BlockSpec((tq, VALUE_DIM), lambda b, h, i: (b * nq + i, h)),
            pl.BlockSpec((seq, VALUE_DIM), lambda b, h, i: (b, h)),
            pl.BlockSpec((seq, VALUE_DIM), lambda b, h, i: (b, h)),
        ],
        out_specs=pl.BlockSpec((tq, VALUE_DIM), lambda b, h, i: (b * nq + i, h)),
        out_shape=jax.ShapeDtypeStruct((t, HEADS * VALUE_DIM), BF16),
        scratch_shapes=[
            pltpu.VMEM((2 * tq, VALUE_DIM), BF16),
            pltpu.VMEM((2, 2 * tq, tq), F32),
            pltpu.VMEM((2, 2 * tq, tq), BF16),
            pltpu.VMEM((2, 2 * tq, LANES), F32),
            pltpu.VMEM((2 * tq, LANES), F32),
            pltpu.VMEM((2 * tq, LANES), F32),
            pltpu.VMEM((2 * tq, VALUE_DIM), F32),
        ],
        compiler_params=pltpu.CompilerParams(dimension_semantics=("arbitrary", "arbitrary", "arbitrary"),
                                             vmem_limit_bytes=VMEM_LIMIT),
        name="diff_attention",
    )(lam_params, subln_g, q, k, v)


def _outproj_kernel(x_ref, u_ref, o_ref, w_ref, g_ref, b_ref, wr_ref, br_ref,
                    x1_ref, ri_ref, rw_ref, *, alpha, cw):
    acc = jnp.dot(u_ref[...], w_ref[0:cw, :], preferred_element_type=F32)
    acc = acc + jnp.dot(o_ref[...], w_ref[cw:, :], preferred_element_type=F32)
    x1 = _layernorm_rows(alpha * x_ref[...] + acc, g_ref[...], b_ref[...])
    x1_ref[...] = x1

    logits = jnp.dot(x1, wr_ref[...], preferred_element_type=F32, precision=lax.Precision.HIGHEST) + br_ref[...]
    lane = lax.broadcasted_iota(jnp.int32, logits.shape, 1)
    neg = -jnp.inf
    gmask = lane < N_GROUPS
    lg = jnp.where(gmask, logits, neg)
    gmax = jnp.max(lg, axis=-1, keepdims=True)
    grp = jnp.min(jnp.where(lg == gmax, lane, LANES), axis=-1, keepdims=True)
    gsum = jnp.sum(jnp.where(gmask, jnp.exp(lg - gmax), 0.0), axis=-1, keepdims=True)
    gate = 1.0 / gsum
    emask = (lane >= N_GROUPS) & (lane < N_GROUPS + N_EXPERTS) & (((lane - N_GROUPS) // EXPERTS_PER_GROUP) == grp)
    ev = jnp.where(emask, logits, neg)
    v0 = jnp.max(ev, axis=-1, keepdims=True)
    i0 = jnp.min(jnp.where(emask & (ev == v0), lane, LANES), axis=-1, keepdims=True)
    emask1 = emask & (lane != i0)
    ev1 = jnp.where(emask1, logits, neg)
    v1 = jnp.max(ev1, axis=-1, keepdims=True)
    i1 = jnp.min(jnp.where(emask1 & (ev1 == v1), lane, LANES), axis=-1, keepdims=True)
    e1 = jnp.exp(v1 - v0)
    w0 = gate * (1.0 / (1.0 + e1))
    w1 = gate * (e1 / (1.0 + e1))
    ri_ref[...] = jnp.where(lane == 0, i0 - N_GROUPS, jnp.where(lane == 1, i1 - N_GROUPS, 0))
    rw_ref[...] = jnp.where(lane == 0, w0, jnp.where(lane == 1, w1, 0.0))


def _outproj_router(x2, u2, o, w_bf, g, b, wr, br, *, tm, alpha):
    t, d = x2.shape
    cw = u2.shape[1]
    row = lambda i: (i, 0)
    const = lambda i: (0, 0)
    return pl.pallas_call(
        functools.partial(_outproj_kernel, alpha=alpha, cw=cw),
        grid=(t // tm,),
        in_specs=[
            pl.BlockSpec((tm, d), row),
            pl.BlockSpec((tm, cw), row),
            pl.BlockSpec((tm, o.shape[1]), row),
            pl.BlockSpec(w_bf.shape, const),
            pl.BlockSpec((1, d), const),
            pl.BlockSpec((1, d), const),
            pl.BlockSpec((d, LANES), const),
            pl.BlockSpec((1, LANES), const),
        ],
        out_specs=[
            pl.BlockSpec((tm, d), row),
            pl.BlockSpec((tm, LANES), row),
            pl.BlockSpec((tm, LANES), row),
        ],
        out_shape=[
            jax.ShapeDtypeStruct((t, d), F32),
            jax.ShapeDtypeStruct((t, LANES), jnp.int32),
            jax.ShapeDtypeStruct((t, LANES), F32),
        ],
        compiler_params=pltpu.CompilerParams(dimension_semantics=("arbitrary",), vmem_limit_bytes=VMEM_LIMIT),
        name="outproj_router",
    )(x2, u2, o, w_bf, g, b, wr, br)


def _row_gather_copy(src_hbm, dst_buf, sem, slot, row, src_row):
    return pltpu.make_async_copy(src_hbm.at[pl.ds(src_row, 1)], dst_buf.at[slot, pl.ds(row, 1)], sem.at[slot])


def _expert_kernel(tok_ref, blk_e_ref, x_hbm, wp_ref, wg_ref, wu_ref, wd_ref, y_ref, xbuf, sem):
    del blk_e_ref
    b = pl.program_id(0)
    nb = pl.num_programs(0)
    slot = b % 2

    def issue(blk, s):
        for r in range(MOE_BLOCK):
            _row_gather_copy(x_hbm, xbuf, sem, s, r, tok_ref[blk * MOE_BLOCK + r]).start()

    @pl.when(b == 0)
    def _():
        issue(0, 0)

    @pl.when(b + 1 < nb)
    def _():
        issue(b + 1, 1 - slot)

    for r in range(MOE_BLOCK):
        _row_gather_copy(x_hbm, xbuf, sem, slot, r, 0).wait()

    xb = xbuf[slot].astype(BF16)
    gate = jnp.dot(xb, wg_ref[...], preferred_element_type=F32)
    up = jnp.dot(xb, wu_ref[...], preferred_element_type=F32)
    h = (gate * jax.nn.sigmoid(gate)) * up
    y = jnp.dot(h.astype(BF16), wd_ref[...], preferred_element_type=F32)
    y_ref[...] = y * wp_ref[...]


def _expert_mlp(tok_pad, blk_e, x1, w_pad, wg_bf, wu_bf, wd_bf):
    n_rows = tok_pad.shape[0]
    n_blocks = n_rows // MOE_BLOCK
    d = x1.shape[1]
    de = wg_bf.shape[2]
    grid_spec = pltpu.PrefetchScalarGridSpec(
        num_scalar_prefetch=2,
        grid=(n_blocks,),
        in_specs=[
            pl.BlockSpec(memory_space=pl.ANY),
            pl.BlockSpec((MOE_BLOCK, 1), lambda i, tok, be: (i, 0)),
            pl.BlockSpec((None, d, de), lambda i, tok, be: (be[i], 0, 0)),
            pl.BlockSpec((None, d, de), lambda i, tok, be: (be[i], 0, 0)),
            pl.BlockSpec((None, de, d), lambda i, tok, be: (be[i], 0, 0)),
        ],
        out_specs=pl.BlockSpec((MOE_BLOCK, d), lambda i, tok, be: (i, 0)),
        scratch_shapes=[pltpu.VMEM((2, MOE_BLOCK, d), F32), pltpu.SemaphoreType.DMA((2,))],
    )
    return pl.pallas_call(
        _expert_kernel,
        grid_spec=grid_spec,
        out_shape=jax.ShapeDtypeStruct((n_rows, d), F32),
        compiler_params=pltpu.CompilerParams(dimension_semantics=("arbitrary",), vmem_limit_bytes=VMEM_LIMIT),
        name="expert_mlp",
    )(tok_pad, blk_e, x1, w_pad, wg_bf, wu_bf, wd_bf)


def _combine_kernel(pos_ref, y_hbm, x_ref, g_ref, b_ref, o_ref, ybuf, sem, *, tm, alpha):
    i = pl.program_id(0)
    n = pl.num_programs(0)
    slot = i % 2

    def issue(tile, s):
        for r in range(2 * tm):
            _row_gather_copy(y_hbm, ybuf, sem, s, r, pos_ref[tile * 2 * tm + r]).start()

    @pl.when(i == 0)
    def _():
        issue(0, 0)

    @pl.when(i + 1 < n)
    def _():
        issue(i + 1, 1 - slot)

    for r in range(2 * tm):
        _row_gather_copy(y_hbm, ybuf, sem, slot, r, 0).wait()

    yy = ybuf[slot]
    f = yy[:tm] + yy[tm:]
    o_ref[...] = _layernorm_rows(alpha * x_ref[...] + f, g_ref[...], b_ref[...])


def _combine(pos, y, x1, g, b, *, tm, alpha):
    t, d = x1.shape
    grid_spec = pltpu.PrefetchScalarGridSpec(
        num_scalar_prefetch=1,
        grid=(t // tm,),
        in_specs=[
            pl.BlockSpec(memory_space=pl.ANY),
            pl.BlockSpec((tm, d), lambda i, pos: (i, 0)),
            pl.BlockSpec((1, d), lambda i, pos: (0, 0)),
            pl.BlockSpec((1, d), lambda i, pos: (0, 0)),
        ],
        out_specs=pl.BlockSpec((tm, d), lambda i, pos: (i, 0)),
        scratch_shapes=[pltpu.VMEM((2, 2 * tm, d), F32), pltpu.SemaphoreType.DMA((2,))],
    )
    return pl.pallas_call(
        functools.partial(_combine_kernel, tm=tm, alpha=alpha),
        grid_spec=grid_spec,
        out_shape=jax.ShapeDtypeStruct((t, d), F32),
        compiler_params=pltpu.CompilerParams(dimension_semantics=("arbitrary",), vmem_limit_bytes=VMEM_LIMIT),
        name="moe_combine",
    )(pos, y, x1, g, b)


def _dispatch_plan(route_i, route_w, *, tm):
    t = route_i.shape[0]
    eid = route_i[:, :2].reshape(-1)
    wts = route_w[:, :2].reshape(-1)
    n_assign = eid.shape[0]
    onehot = (eid[:, None] == jnp.arange(N_EXPERTS, dtype=jnp.int32)[None, :]).astype(jnp.int32)
    csum = jnp.cumsum(onehot, axis=0)
    rank = jnp.take_along_axis(csum, eid[:, None], axis=1)[:, 0] - 1
    counts = csum[-1]
    padded = ((counts + MOE_BLOCK - 1) // MOE_BLOCK) * MOE_BLOCK
    pad_end = jnp.cumsum(padded)
    pad_start = pad_end - padded
    dest = pad_start[eid] + rank
    n_blocks = -(-n_assign // MOE_BLOCK) + N_EXPERTS
    n_rows = n_blocks * MOE_BLOCK
    tok = jnp.arange(n_assign, dtype=jnp.int32) // 2
    tok_pad = jnp.zeros((n_rows,), jnp.int32).at[dest].set(tok)
    w_pad = jnp.zeros((n_rows,), F32).at[dest].set(wts)
    blk_e = jnp.minimum(jnp.searchsorted(pad_end, jnp.arange(n_blocks, dtype=jnp.int32) * MOE_BLOCK, side="right"),
                        N_EXPERTS - 1).astype(jnp.int32)
    pos = dest.reshape(t // tm, tm, 2).transpose(0, 2, 1).reshape(-1).astype(jnp.int32)
    return tok_pad, w_pad.reshape(n_rows, 1), blk_e, pos


def _rope_tables(seq):
    half = HEAD_DIM // 2
    inv_freq = 1.0 / (ROPE_THETA ** (jnp.arange(half, dtype=F32) * 2.0 / HEAD_DIM))
    ang = jnp.arange(seq, dtype=F32)[:, None] * inv_freq[None, :]
    cos, sin = jnp.cos(ang), jnp.sin(ang)
    reps = LANES // HEAD_DIM
    return jnp.tile(jnp.concatenate([cos, cos], axis=1), (1, reps)), jnp.tile(jnp.concatenate([-sin, sin], axis=1), (1, reps))


def _pick_tile(n, want):
    while n % want:
        want //= 2
    return want


def kernel(x, w_in, b_in, conv_w, conv_b, conv_ln_g, conv_ln_b, lam_q1, lam_k1, lam_q2, lam_k2, subln_g, w_out, ln1_g, ln1_b, w_rg, b_rg, w_re, b_re, w_gate_e, w_up_e, w_down_e, ln2_g, ln2_b):
    batch, seq, d = x.shape
    depth = w_in.shape[0]
    t = batch * seq
    alpha = (2.0 * depth) ** 0.25
    cos_t, sin_t = _rope_tables(seq)
    tm = _pick_tile(seq, 512)
    ts = _pick_tile(seq, 256)
    tq = _pick_tile(seq, 512)
    tc = _pick_tile(seq, 256)
    row = lambda a: a.reshape(1, -1)

    x2 = x.reshape(t, d)
    for l in range(depth):
        lam_init = 0.8 - 0.6 * math.exp(-0.3 * l)
        u, q, k, v = _inproj(x2, w_in[l].astype(BF16), row(b_in[l]), cos_t, sin_t, seq=seq, tm=tm)
        u2 = _conv_module(u, conv_w[l].reshape(CONV_KERNEL, -1), row(conv_b[l]), row(conv_ln_g[l]),
                          row(conv_ln_b[l]), seq=seq, ts=ts)
        lam_params = jnp.stack([lam_q1[l], lam_k1[l], lam_q2[l], lam_k2[l]])
        o = _diff_attention(q, k, v, lam_params, row(subln_g[l]), batch=batch, seq=seq, tq=tq, lam_init=lam_init)
        w_router = jnp.concatenate([w_rg[l], w_re[l]], axis=1)
        w_router = jnp.pad(w_router, ((0, 0), (0, LANES - w_router.shape[1])))
        b_router = jnp.pad(jnp.concatenate([b_rg[l], b_re[l]]), (0, LANES - N_GROUPS - N_EXPERTS))
        x1, route_i, route_w = _outproj_router(x2, u2, o, w_out[l].astype(BF16), row(ln1_g[l]), row(ln1_b[l]),
                                               w_router, row(b_router), tm=tm, alpha=alpha)
        tok_pad, w_pad, blk_e, pos = _dispatch_plan(route_i, route_w, tm=tc)
        y = _expert_mlp(tok_pad, blk_e, x1, w_pad, w_gate_e[l].astype(BF16), w_up_e[l].astype(BF16),
                        w_down_e[l].astype(BF16))
        x2 = _combine(pos, y, x1, row(ln2_g[l]), row(ln2_b[l]), tm=tc, alpha=alpha)
    return x2.reshape(batch, seq, d)
```

```python
import functools
import math

import jax
import jax.numpy as jnp
from jax import lax
from jax.experimental import pallas as pl
from jax.experimental.pallas import tpu as pltpu

F32 = jnp.float32
BF16 = jnp.bfloat16

CHUNK = 64
CONV_KERNEL = 31
HEADS = 4
HEAD_DIM = 64
VALUE_DIM = 2 * HEAD_DIM
ROPE_THETA = 10000.0
N_GROUPS = 4
EXPERTS_PER_GROUP = 8
N_EXPERTS = N_GROUPS * EXPERTS_PER_GROUP
MOE_BLOCK = 128
LN_EPS = 1e-5
LANES = 128
CONV_HALO = 32
VMEM_LIMIT = 56 * 1024 * 1024


def _layernorm_rows(y, g, b):
    mu = jnp.mean(y, axis=-1, keepdims=True)
    d = y - mu
    var = jnp.mean(d * d, axis=-1, keepdims=True)
    return d * lax.rsqrt(var + LN_EPS) * g + b


def _inproj_kernel(x_ref, w_ref, b_ref, cos_ref, sin_ref, u_ref, q_ref, k_ref, v_ref, *, cw, qk):
    xb = x_ref[...].astype(BF16)

    def proj(c0, n):
        return jnp.dot(xb, w_ref[:, c0:c0 + n], preferred_element_type=F32) + b_ref[:, c0:c0 + n]

    ag = proj(0, 2 * cw)
    u_ref[...] = ag[:, :cw] * jax.nn.sigmoid(ag[:, cw:])

    cos = cos_ref[...]
    sin = sin_ref[...]
    lane = lax.broadcasted_iota(jnp.int32, cos.shape, 1)
    first_half = (lane % HEAD_DIM) < (HEAD_DIM // 2)

    def rope(z, scale):
        outs = []
        for c in range(0, qk, LANES):
            zz = z[:, c:c + LANES]
            swapped = jnp.where(first_half, pltpu.roll(zz, LANES - HEAD_DIM // 2, 1),
                                pltpu.roll(zz, HEAD_DIM // 2, 1))
            outs.append((zz * cos + swapped * sin) * scale)
        return jnp.concatenate(outs, axis=1)

    q_ref[...] = rope(proj(2 * cw, qk), HEAD_DIM ** -0.5 * math.log2(math.e)).astype(BF16)
    k_ref[...] = rope(proj(2 * cw + qk, qk), 1.0).astype(BF16)
    v_ref[...] = proj(2 * cw + 2 * qk, v_ref.shape[1]).astype(BF16)


def _inproj(x2, w_bf, b, cos_t, sin_t, *, seq, tm):
    t, d = x2.shape
    n = w_bf.shape[1]
    cw = d // 2
    qk = 2 * HEADS * HEAD_DIM
    vw = HEADS * VALUE_DIM
    tiles_per_seq = seq // tm
    return pl.pallas_call(
        functools.partial(_inproj_kernel, cw=cw, qk=qk),
        grid=(t // tm,),
        in_specs=[
            pl.BlockSpec((tm, d), lambda i: (i, 0)),
            pl.BlockSpec((d, n), lambda i: (0, 0)),
            pl.BlockSpec((1, n), lambda i: (0, 0)),
            pl.BlockSpec((tm, LANES), lambda i: (i % tiles_per_seq, 0)),
            pl.BlockSpec((tm, LANES), lambda i: (i % tiles_per_seq, 0)),
        ],
        out_specs=[
            pl.BlockSpec((tm, cw), lambda i: (i, 0)),
            pl.BlockSpec((tm, qk), lambda i: (i, 0)),
            pl.BlockSpec((tm, qk), lambda i: (i, 0)),
            pl.BlockSpec((tm, vw), lambda i: (i, 0)),
        ],
        out_shape=[
            jax.ShapeDtypeStruct((t, cw), F32),
            jax.ShapeDtypeStruct((t, qk), BF16),
            jax.ShapeDtypeStruct((t, qk), BF16),
            jax.ShapeDtypeStruct((t, vw), BF16),
        ],
        compiler_params=pltpu.CompilerParams(dimension_semantics=("arbitrary",), vmem_limit_bytes=VMEM_LIMIT),
        name="inproj",
    )(x2, w_bf, b, cos_t, sin_t)


def _conv_kernel(u_ref, w_ref, b_ref, g_ref, beta_ref, o_ref, buf_ref, *, ts, tiles_per_seq, rb):
    i = pl.program_id(0)

    @pl.when(i % tiles_per_seq == 0)
    def _():
        buf_ref[0:CONV_HALO, :] = jnp.zeros((CONV_HALO, buf_ref.shape[1]), F32)

    @pl.when(i % tiles_per_seq != 0)
    def _():
        buf_ref[0:CONV_HALO, :] = buf_ref[ts:ts + CONV_HALO, :]

    buf_ref[CONV_HALO:CONV_HALO + ts, :] = u_ref[...]

    bias = b_ref[...]
    g = g_ref[...]
    beta = beta_ref[...]
    first = CONV_HALO - (CONV_KERNEL - 1)
    for r0 in range(0, ts, rb):
        acc = jnp.broadcast_to(bias, (rb, bias.shape[1]))
        for j in range(CONV_KERNEL):
            acc = acc + w_ref[j:j + 1, :] * buf_ref[r0 + first + j:r0 + first + j + rb, :]
        y = _layernorm_rows(acc, g, beta)
        o_ref[r0:r0 + rb, :] = (y * jax.nn.sigmoid(y)).astype(o_ref.dtype)


def _conv_module(u, w, b, g, beta, *, seq, ts, rb=32):
    t, c = u.shape
    tiles_per_seq = seq // ts
    return pl.pallas_call(
        functools.partial(_conv_kernel, ts=ts, tiles_per_seq=tiles_per_seq, rb=rb),
        grid=(t // ts,),
        in_specs=[
            pl.BlockSpec((ts, c), lambda i: (i, 0)),
            pl.BlockSpec((CONV_KERNEL, c), lambda i: (0, 0)),
            pl.BlockSpec((1, c), lambda i: (0, 0)),
            pl.BlockSpec((1, c), lambda i: (0, 0)),
            pl.BlockSpec((1, c), lambda i: (0, 0)),
        ],
        out_specs=pl.BlockSpec((ts, c), lambda i: (i, 0)),
        out_shape=jax.ShapeDtypeStruct((t, c), BF16),
        scratch_shapes=[pltpu.VMEM((CONV_HALO + ts, c), F32)],
        compiler_params=pltpu.CompilerParams(dimension_semantics=("arbitrary",)),
        name="conv_module",
    )(u, w, b, g, beta)


def _attn_kernel(lam_ref, g_ref, q_ref, k_ref, v_ref, o_ref, qq_scr, s_scr, p_scr, a_scr, m_scr, l_scr, acc_scr,
                 *, tq, rc, lam_init):
    qi = pl.program_id(2)
    q = q_ref[...]
    lane = lax.broadcasted_iota(jnp.int32, q.shape, 1)
    zero = jnp.zeros_like(q)
    qq_scr[0:tq, :] = jnp.where(lane < HEAD_DIM, q, zero)
    qq_scr[tq:, :] = jnp.where(lane >= HEAD_DIM, q, zero)

    m_scr[...] = jnp.full(m_scr.shape, -jnp.inf, F32)
    l_scr[...] = jnp.zeros(l_scr.shape, F32)
    acc_scr[...] = jnp.zeros(acc_scr.shape, F32)

    def kv_rows(j):
        return pl.ds(pl.multiple_of(j * tq, tq), tq)

    def scores(j, slot):
        s_scr[slot] = lax.dot_general(qq_scr[...], k_ref[kv_rows(j), :], (((1,), (1,)), ((), ())),
                                      preferred_element_type=F32)

    def softmax(slot, masked):
        for r0 in range(0, 2 * tq, rc):
            s = s_scr[slot, r0:r0 + rc, :]
            if masked:
                row = (lax.broadcasted_iota(jnp.int32, s.shape, 0) + r0) % tq
                col = lax.broadcasted_iota(jnp.int32, s.shape, 1)
                s = jnp.where((col // CHUNK) <= (row // CHUNK), s, -jnp.inf)
            m_prev = m_scr[r0:r0 + rc, :]
            m_new = jnp.maximum(m_prev, jnp.max(s, axis=-1, keepdims=True))
            alpha = jnp.exp2(m_prev - m_new)
            p = jnp.exp2(s - jnp.tile(m_new, (1, tq // LANES)))
            l_scr[r0:r0 + rc, :] = alpha * l_scr[r0:r0 + rc, :] + jnp.sum(p, axis=-1, keepdims=True)
            m_scr[r0:r0 + rc, :] = m_new
            a_scr[slot, r0:r0 + rc, :] = alpha
            p_scr[slot, r0:r0 + rc, :] = p.astype(BF16)

    def values(j, slot):
        acc_scr[...] = a_scr[slot] * acc_scr[...] + jnp.dot(p_scr[slot], v_ref[kv_rows(j), :],
                                                            preferred_element_type=F32)

    scores(0, 0)

    @pl.when(qi > 0)
    def _():
        scores(1, 1)
        softmax(0, False)

    def body(i, carry):
        for par in (0, 1):
            @pl.when(i % 2 == par)
            def _():
                scores(i + 1, 1 - par)
                softmax(par, False)
                values(i - 1, 1 - par)
        return carry

    lax.fori_loop(1, qi, body, 0)

    for par in (0, 1):
        @pl.when((qi > 0) & (qi % 2 == par))
        def _():
            softmax(par, True)
            values(qi - 1, 1 - par)
            values(qi, par)

    @pl.when(qi == 0)
    def _():
        softmax(0, True)
        values(0, 0)

    lp = lam_ref[...]
    lam = (jnp.exp(jnp.sum(lp[0:1] * lp[1:2], axis=-1, keepdims=True))
           - jnp.exp(jnp.sum(lp[2:3] * lp[3:4], axis=-1, keepdims=True)) + lam_init)
    o = acc_scr[...] / l_scr[...]
    o = o[:tq] - lam * o[tq:]
    ms = jnp.mean(o * o, axis=-1, keepdims=True)
    o_ref[...] = (o * lax.rsqrt(ms + LN_EPS) * g_ref[...] * (1.0 - lam_init)).astype(o_ref.dtype)


def _diff_attention(q, k, v, lam_params, subln_g, *, batch, seq, tq, lam_init):
    t = q.shape[0]
    nq = seq // tq
    return pl.pallas_call(
        functools.partial(_attn_kernel, tq=tq, rc=min(256, tq), lam_init=lam_init),
        grid=(batch, HEADS, nq),
        in_specs=[
            pl.BlockSpec((4, HEAD_DIM), lambda b, h, i: (0, 0)),
            pl.BlockSpec((1, VALUE_DIM), lambda b, h, i: (0, 0)),
            pl.BlockSpec((tq, VALUE_DIM), lambda b, h, i: (b * nq + i, h)),
            pl.BlockSpec((seq, VALUE_DIM), lambda b, h, i: (b, h)),
            pl.BlockSpec((seq, VALUE_DIM), lambda b, h, i: (b, h)),
        ],
        out_specs=pl.BlockSpec((tq, VALUE_DIM), lambda b, h, i: (b * nq + i, h)),
        out_shape=jax.ShapeDtypeStruct((t, HEADS * VALUE_DIM), BF16),
        scratch_shapes=[
            pltpu.VMEM((2 * tq, VALUE_DIM), BF16),
            pltpu.VMEM((2, 2 * tq, tq), F32),
            pltpu.VMEM((2, 2 * tq, tq), BF16),
            pltpu.VMEM((2, 2 * tq, LANES), F32),
            pltpu.VMEM((2 * tq, LANES), F32),
            pltpu.VMEM((2 * tq, LANES), F32),
            pltpu.VMEM((2 * tq, VALUE_DIM), F32),
        ],
        compiler_params=pltpu.CompilerParams(dimension_semantics=("arbitrary", "arbitrary", "arbitrary"),
                                             vmem_limit_bytes=VMEM_LIMIT),
        name="diff_attention",
    )(lam_params, subln_g, q, k, v)


def _outproj_kernel(x_ref, u_ref, o_ref, w_ref, g_ref, b_ref, wr_ref, br_ref,
                    x1_ref, ri_ref, rw_ref, cnt_ref, carry_scr, *, alpha, cw):
    acc = jnp.dot(u_ref[...], w_ref[0:cw, :], preferred_element_type=F32)
    acc = acc + jnp.dot(o_ref[...], w_ref[cw:, :], preferred_element_type=F32)
    x1 = _layernorm_rows(alpha * x_ref[...] + acc, g_ref[...], b_ref[...])
    x1_ref[...] = x1

    logits = jnp.dot(x1, wr_ref[...], preferred_element_type=F32, precision=lax.Precision.HIGHEST) + br_ref[...]
    lane = lax.broadcasted_iota(jnp.int32, logits.shape, 1)
    neg = -jnp.inf
    gmask = lane < N_GROUPS
    lg = jnp.where(gmask, logits, neg)
    gmax = jnp.max(lg, axis=-1, keepdims=True)
    grp = jnp.min(jnp.where(lg == gmax, lane, LANES), axis=-1, keepdims=True)
    gsum = jnp.sum(jnp.where(gmask, jnp.exp(lg - gmax), 0.0), axis=-1, keepdims=True)
    gate = 1.0 / gsum
    emask = (lane >= N_GROUPS) & (lane < N_GROUPS + N_EXPERTS) & (((lane - N_GROUPS) // EXPERTS_PER_GROUP) == grp)
    ev = jnp.where(emask, logits, neg)
    v0 = jnp.max(ev, axis=-1, keepdims=True)
    i0 = jnp.min(jnp.where(emask & (ev == v0), lane, LANES), axis=-1, keepdims=True)
    emask1 = emask & (lane != i0)
    ev1 = jnp.where(emask1, logits, neg)
    v1 = jnp.max(ev1, axis=-1, keepdims=True)
    i1 = jnp.min(jnp.where(emask1 & (ev1 == v1), lane, LANES), axis=-1, keepdims=True)
    e1 = jnp.exp(v1 - v0)
    w0 = gate * (1.0 / (1.0 + e1))
    w1 = gate * (e1 / (1.0 + e1))

    @pl.when(pl.program_id(0) == 0)
    def _():
        carry_scr[...] = jnp.zeros(carry_scr.shape, F32)

    tm = logits.shape[0]
    oh0 = (lane == i0 - N_GROUPS).astype(F32)
    oh1 = (lane == i1 - N_GROUPS).astype(F32)
    both = oh0 + oh1
    tri = (lax.broadcasted_iota(jnp.int32, (tm, tm), 0) > lax.broadcasted_iota(jnp.int32, (tm, tm), 1)).astype(BF16)
    before = jnp.dot(tri, both.astype(BF16), preferred_element_type=F32) + carry_scr[...]
    rank0 = jnp.sum(oh0 * before, axis=-1, keepdims=True).astype(jnp.int32)
    rank1 = jnp.sum(oh1 * before, axis=-1, keepdims=True).astype(jnp.int32)
    carry_scr[...] = carry_scr[...] + jnp.sum(both, axis=0, keepdims=True)
    cnt_ref[...] = carry_scr[...]

    ri_ref[...] = jnp.where(lane == 0, i0 - N_GROUPS, jnp.where(lane == 1, i1 - N_GROUPS,
                            jnp.where(lane == 2, rank0, jnp.where(lane == 3, rank1, 0))))
    rw_ref[...] = jnp.where(lane == 0, w0, jnp.where(lane == 1, w1, 0.0))


def _outproj_router(x2, u2, o, w_bf, g, b, wr, br, *, tm, alpha):
    t, d = x2.shape
    cw = u2.shape[1]
    row = lambda i: (i, 0)
    const = lambda i: (0, 0)
    return pl.pallas_call(
        functools.partial(_outproj_kernel, alpha=alpha, cw=cw),
        grid=(t // tm,),
        in_specs=[
            pl.BlockSpec((tm, d), row),
            pl.BlockSpec((tm, cw), row),
            pl.BlockSpec((tm, o.shape[1]), row),
            pl.BlockSpec(w_bf.shape, const),
            pl.BlockSpec((1, d), const),
            pl.BlockSpec((1, d), const),
            pl.BlockSpec((d, LANES), const),
            pl.BlockSpec((1, LANES), const),
        ],
        out_specs=[
            pl.BlockSpec((tm, d), row),
            pl.BlockSpec((tm, LANES), row),
            pl.BlockSpec((tm, LANES), row),
            pl.BlockSpec((1, LANES), const),
        ],
        out_shape=[
            jax.ShapeDtypeStruct((t, d), F32),
            jax.ShapeDtypeStruct((t, LANES), jnp.int32),
            jax.ShapeDtypeStruct((t, LANES), F32),
            jax.ShapeDtypeStruct((1, LANES), F32),
        ],
        scratch_shapes=[pltpu.VMEM((1, LANES), F32)],
        compiler_params=pltpu.CompilerParams(dimension_semantics=("arbitrary",), vmem_limit_bytes=VMEM_LIMIT),
        name="outproj_router",
    )(x2, u2, o, w_bf, g, b, wr, br)


def _dispatch_kernel(pos_ref, pend_ref, x_ref, xs_hbm, zbuf, sem, zsem, *, tm):
    i = pl.program_id(0)

    def zero_copy(e):
        start = pl.multiple_of(pend_ref[e + 1] - MOE_BLOCK, MOE_BLOCK)
        return pltpu.make_async_copy(zbuf, xs_hbm.at[pl.ds(start, MOE_BLOCK)], zsem)

    def zero_block(b):
        return pltpu.make_async_copy(zbuf, xs_hbm.at[pl.ds(pl.multiple_of(b * MOE_BLOCK, MOE_BLOCK), MOE_BLOCK)], zsem)

    @pl.when(i == 0)
    def _():
        zbuf[...] = jnp.zeros(zbuf.shape, zbuf.dtype)
        first_unused = pend_ref[N_EXPERTS] // MOE_BLOCK
        n_blocks = xs_hbm.shape[0] // MOE_BLOCK
        for e in range(N_EXPERTS):
            @pl.when(pend_ref[e + 1] > pend_ref[e])
            def _():
                zero_copy(e).start()

        def start_unused(b, c):
            zero_block(b).start()
            return c

        def wait_unused(b, c):
            zero_block(b).wait()
            return c

        lax.fori_loop(first_unused, n_blocks, start_unused, 0)
        for e in range(N_EXPERTS):
            @pl.when(pend_ref[e + 1] > pend_ref[e])
            def _():
                zero_copy(e).wait()
        lax.fori_loop(first_unused, n_blocks, wait_unused, 0)

    def row_copy(r, dst_row):
        return pltpu.make_async_copy(x_ref.at[pl.ds(r % tm, 1)], xs_hbm.at[pl.ds(dst_row, 1)], sem)

    for r in range(2 * tm):
        row_copy(r, pos_ref[i * 2 * tm + r]).start()
    for r in range(2 * tm):
        row_copy(r, 0).wait()


def _dispatch(pos, pad_ends, x1, *, n_rows, tm):
    t, d = x1.shape
    grid_spec = pltpu.PrefetchScalarGridSpec(
        num_scalar_prefetch=2,
        grid=(t // tm,),
        in_specs=[pl.BlockSpec((tm, d), lambda i, pos, pe: (i, 0))],
        out_specs=pl.BlockSpec(memory_space=pl.ANY),
        scratch_shapes=[pltpu.VMEM((MOE_BLOCK, d), x1.dtype), pltpu.SemaphoreType.DMA(()), pltpu.SemaphoreType.DMA(())],
    )
    return pl.pallas_call(
        functools.partial(_dispatch_kernel, tm=tm),
        grid_spec=grid_spec,
        out_shape=jax.ShapeDtypeStruct((n_rows, d), x1.dtype),
        compiler_params=pltpu.CompilerParams(dimension_semantics=("arbitrary",)),
        name="moe_dispatch",
    )(pos, pad_ends, x1)


def _expert_kernel(blk_e_ref, nused_ref, xs_ref, wg_ref, wu_ref, wd_ref, y_ref, wg_bf, wu_bf, wd_bf):
    i = pl.program_id(0)

    @pl.when(i < nused_ref[0])
    def _():
        @pl.when((i == 0) | (blk_e_ref[i] != blk_e_ref[jnp.maximum(i - 1, 0)]))
        def _():
            wg_bf[...] = wg_ref[...].astype(BF16)
            wu_bf[...] = wu_ref[...].astype(BF16)
            wd_bf[...] = wd_ref[...].astype(BF16)

        xb = xs_ref[...].astype(BF16)
        gate = jnp.dot(xb, wg_bf[...], preferred_element_type=F32)
        up = jnp.dot(xb, wu_bf[...], preferred_element_type=F32)
        h = (gate * jax.nn.sigmoid(gate)) * up
        y_ref[...] = jnp.dot(h.astype(BF16), wd_bf[...], preferred_element_type=F32)

    @pl.when(i >= nused_ref[0])
    def _():
        y_ref[...] = jnp.zeros(y_ref.shape, y_ref.dtype)


def _expert_mlp(blk_e, n_used, xs, wg, wu, wd, *, layer):
    n_rows, d = xs.shape
    n_blocks = n_rows // MOE_BLOCK
    de = wg.shape[3]
    live = lambda i, be, nu: jnp.minimum(i, nu[0] - 1)
    grid_spec = pltpu.PrefetchScalarGridSpec(
        num_scalar_prefetch=2,
        grid=(n_blocks,),
        in_specs=[
            pl.BlockSpec((MOE_BLOCK, d), lambda i, be, nu: (live(i, be, nu), 0)),
            pl.BlockSpec((None, None, d, de), lambda i, be, nu: (layer, be[live(i, be, nu)], 0, 0)),
            pl.BlockSpec((None, None, d, de), lambda i, be, nu: (layer, be[live(i, be, nu)], 0, 0)),
            pl.BlockSpec((None, None, de, d), lambda i, be, nu: (layer, be[live(i, be, nu)], 0, 0)),
        ],
        out_specs=pl.BlockSpec((MOE_BLOCK, d), lambda i, be, nu: (i, 0)),
        scratch_shapes=[pltpu.VMEM((d, de), BF16), pltpu.VMEM((d, de), BF16), pltpu.VMEM((de, d), BF16)],
    )
    return pl.pallas_call(
        _expert_kernel,
        grid_spec=grid_spec,
        out_shape=jax.ShapeDtypeStruct((n_rows, d), F32),
        compiler_params=pltpu.CompilerParams(dimension_semantics=("arbitrary",), vmem_limit_bytes=VMEM_LIMIT),
        name="expert_mlp",
    )(blk_e, n_used, xs, wg, wu, wd)


def _row_gather_copy(src_hbm, dst_buf, sem, slot, row, src_row):
    return pltpu.make_async_copy(src_hbm.at[pl.ds(src_row, 1)], dst_buf.at[slot, pl.ds(row, 1)], sem.at[slot])


def _combine_kernel(pos_ref, y_hbm, x_ref, rw_ref, g_ref, b_ref, o_ref, ybuf, sem, *, tm, alpha):
    i = pl.program_id(0)
    n = pl.num_programs(0)
    slot = i % 2

    def issue(tile, s):
        for r in range(2 * tm):
            _row_gather_copy(y_hbm, ybuf, sem, s, r, pos_ref[tile * 2 * tm + r]).start()

    @pl.when(i == 0)
    def _():
        issue(0, 0)

    @pl.when(i + 1 < n)
    def _():
        issue(i + 1, 1 - slot)

    for r in range(2 * tm):
        _row_gather_copy(y_hbm, ybuf, sem, slot, r, 0).wait()

    yy = ybuf[slot]
    rw = rw_ref[...]
    f = yy[:tm] * rw[:, 0:1] + yy[tm:] * rw[:, 1:2]
    o_ref[...] = _layernorm_rows(alpha * x_ref[...] + f, g_ref[...], b_ref[...])


def _combine(pos, y, x1, route_w, g, b, *, tm, alpha):
    t, d = x1.shape
    grid_spec = pltpu.PrefetchScalarGridSpec(
        num_scalar_prefetch=1,
        grid=(t // tm,),
        in_specs=[
            pl.BlockSpec(memory_space=pl.ANY),
            pl.BlockSpec((tm, d), lambda i, pos: (i, 0)),
            pl.BlockSpec((tm, LANES), lambda i, pos: (i, 0)),
            pl.BlockSpec((1, d), lambda i, pos: (0, 0)),
            pl.BlockSpec((1, d), lambda i, pos: (0, 0)),
        ],
        out_specs=pl.BlockSpec((tm, d), lambda i, pos: (i, 0)),
        scratch_shapes=[pltpu.VMEM((2, 2 * tm, d), F32), pltpu.SemaphoreType.DMA((2,))],
    )
    return pl.pallas_call(
        functools.partial(_combine_kernel, tm=tm, alpha=alpha),
        grid_spec=grid_spec,
        out_shape=jax.ShapeDtypeStruct((t, d), F32),
        compiler_params=pltpu.CompilerParams(dimension_semantics=("arbitrary",), vmem_limit_bytes=VMEM_LIMIT),
        name="moe_combine",
    )(pos, y, x1, route_w, g, b)


def _dispatch_plan(route_i, counts, *, tm):
    t = route_i.shape[0]
    counts = counts[0, :N_EXPERTS].astype(jnp.int32)
    padded = ((counts + MOE_BLOCK - 1) // MOE_BLOCK) * MOE_BLOCK
    pad_end = jnp.cumsum(padded)
    pad_start = pad_end - padded
    dest = pad_start[route_i[:, 0:2]] + route_i[:, 2:4]
    pos = dest.reshape(t // tm, tm, 2).transpose(0, 2, 1).reshape(-1).astype(jnp.int32)
    n_blocks = -(-(2 * t) // MOE_BLOCK) + N_EXPERTS
    blk_e = jnp.minimum(jnp.searchsorted(pad_end, jnp.arange(n_blocks, dtype=jnp.int32) * MOE_BLOCK, side="right"),
                        N_EXPERTS - 1).astype(jnp.int32)
    pad_ends = jnp.concatenate([jnp.zeros((1,), jnp.int32), pad_end.astype(jnp.int32)])
    n_used = (pad_end[-1:] // MOE_BLOCK).astype(jnp.int32)
    return pos, pad_ends, blk_e, n_used, n_blocks * MOE_BLOCK


def _rope_tables(seq):
    half = HEAD_DIM // 2
    inv_freq = 1.0 / (ROPE_THETA ** (jnp.arange(half, dtype=F32) * 2.0 / HEAD_DIM))
    ang = jnp.arange(seq, dtype=F32)[:, None] * inv_freq[None, :]
    cos, sin = jnp.cos(ang), jnp.sin(ang)
    reps = LANES // HEAD_DIM
    return jnp.tile(jnp.concatenate([cos, cos], axis=1), (1, reps)), jnp.tile(jnp.concatenate([-sin, sin], axis=1), (1, reps))


def _pick_tile(n, want):
    while n % want:
        want //= 2
    return want


def kernel(x, w_in, b_in, conv_w, conv_b, conv_ln_g, conv_ln_b, lam_q1, lam_k1, lam_q2, lam_k2, subln_g, w_out, ln1_g, ln1_b, w_rg, b_rg, w_re, b_re, w_gate_e, w_up_e, w_down_e, ln2_g, ln2_b):
    batch, seq, d = x.shape
    depth = w_in.shape[0]
    t = batch * seq
    alpha = (2.0 * depth) ** 0.25
    cos_t, sin_t = _rope_tables(seq)
    tm = _pick_tile(seq, 512)
    ts = _pick_tile(seq, 256)
    tq = _pick_tile(seq, 512)
    tc = _pick_tile(seq, 256)
    row = lambda a: a.reshape(1, -1)

    x2 = x.reshape(t, d)
    for l in range(depth):
        lam_init = 0.8 - 0.6 * math.exp(-0.3 * l)
        u, q, k, v = _inproj(x2, w_in[l].astype(BF16), row(b_in[l]), cos_t, sin_t, seq=seq, tm=tm)
        u2 = _conv_module(u, conv_w[l].reshape(CONV_KERNEL, -1), row(conv_b[l]), row(conv_ln_g[l]),
                          row(conv_ln_b[l]), seq=seq, ts=ts)
        lam_params = jnp.stack([lam_q1[l], lam_k1[l], lam_q2[l], lam_k2[l]])
        o = _diff_attention(q, k, v, lam_params, row(subln_g[l]), batch=batch, seq=seq, tq=tq, lam_init=lam_init)
        w_router = jnp.concatenate([w_rg[l], w_re[l]], axis=1)
        w_router = jnp.pad(w_router, ((0, 0), (0, LANES - w_router.shape[1])))
        b_router = jnp.pad(jnp.concatenate([b_rg[l], b_re[l]]), (0, LANES - N_GROUPS - N_EXPERTS))
        x1, route_i, route_w, counts = _outproj_router(x2, u2, o, w_out[l].astype(BF16), row(ln1_g[l]), row(ln1_b[l]),
                                                       w_router, row(b_router), tm=tm, alpha=alpha)
        pos, pad_ends, blk_e, n_used, n_rows = _dispatch_plan(route_i, counts, tm=tc)
        xs = _dispatch(pos, pad_ends, x1, n_rows=n_rows, tm=tc)
        y = _expert_mlp(blk_e, n_used, xs, w_gate_e, w_up_e, w_down_e, layer=l)
        x2 = _combine(pos, y, x1, route_w, row(ln2_g[l]), row(ln2_b[l]), tm=tc, alpha=alpha)
    return x2.reshape(batch, seq, d)
```

```python
import functools
import math

import jax
import jax.numpy as jnp
from jax import lax
from jax.experimental import pallas as pl
from jax.experimental.pallas import tpu as pltpu

F32 = jnp.float32
BF16 = jnp.bfloat16

CHUNK = 64
CONV_KERNEL = 31
HEADS = 4
HEAD_DIM = 64
VALUE_DIM = 2 * HEAD_DIM
ROPE_THETA = 10000.0
N_GROUPS = 4
EXPERTS_PER_GROUP = 8
N_EXPERTS = N_GROUPS * EXPERTS_PER_GROUP
MOE_BLOCK = 256
LN_EPS = 1e-5
LANES = 128
SUBLANES = 8
CONV_HALO = 32
VMEM_LIMIT = 56 * 1024 * 1024


def _layernorm_rows(y, g, b):
    mu = jnp.mean(y, axis=-1, keepdims=True)
    d = y - mu
    var = jnp.mean(d * d, axis=-1, keepdims=True)
    return d * lax.rsqrt(var + LN_EPS) * g + b


def _inproj_kernel(x_ref, w_ref, b_ref, cos_ref, sin_ref, u_ref, q_ref, k_ref, v_ref, *, cw, qk):
    xb = x_ref[...].astype(BF16)

    def proj(c0, n):
        return jnp.dot(xb, w_ref[:, c0:c0 + n], preferred_element_type=F32) + b_ref[:, c0:c0 + n]

    ag = proj(0, 2 * cw)
    u_ref[...] = ag[:, :cw] * jax.nn.sigmoid(ag[:, cw:])

    cos = cos_ref[...]
    sin = sin_ref[...]
    lane = lax.broadcasted_iota(jnp.int32, cos.shape, 1)
    first_half = (lane % HEAD_DIM) < (HEAD_DIM // 2)

    def rope(z, scale):
        outs = []
        for c in range(0, qk, LANES):
            zz = z[:, c:c + LANES]
            swapped = jnp.where(first_half, pltpu.roll(zz, LANES - HEAD_DIM // 2, 1),
                                pltpu.roll(zz, HEAD_DIM // 2, 1))
            outs.append((zz * cos + swapped * sin) * scale)
        return jnp.concatenate(outs, axis=1)

    q_ref[...] = rope(proj(2 * cw, qk), HEAD_DIM ** -0.5 * math.log2(math.e)).astype(BF16)
    k_ref[...] = rope(proj(2 * cw + qk, qk), 1.0).astype(BF16)
    v_ref[...] = proj(2 * cw + 2 * qk, v_ref.shape[1]).astype(BF16)


def _inproj(x2, w_bf, b, cos_t, sin_t, *, seq, tm):
    t, d = x2.shape
    n = w_bf.shape[1]
    cw = d // 2
    qk = 2 * HEADS * HEAD_DIM
    vw = HEADS * VALUE_DIM
    tiles_per_seq = seq // tm
    return pl.pallas_call(
        functools.partial(_inproj_kernel, cw=cw, qk=qk),
        grid=(t // tm,),
        in_specs=[
            pl.BlockSpec((tm, d), lambda i: (i, 0)),
            pl.BlockSpec((d, n), lambda i: (0, 0)),
            pl.BlockSpec((1, n), lambda i: (0, 0)),
            pl.BlockSpec((tm, LANES), lambda i: (i % tiles_per_seq, 0)),
            pl.BlockSpec((tm, LANES), lambda i: (i % tiles_per_seq, 0)),
        ],
        out_specs=[
            pl.BlockSpec((tm, cw), lambda i: (i, 0)),
            pl.BlockSpec((tm, qk), lambda i: (i, 0)),
            pl.BlockSpec((tm, qk), lambda i: (i, 0)),
            pl.BlockSpec((tm, vw), lambda i: (i, 0)),
        ],
        out_shape=[
            jax.ShapeDtypeStruct((t, cw), F32),
            jax.ShapeDtypeStruct((t, qk), BF16),
            jax.ShapeDtypeStruct((t, qk), BF16),
            jax.ShapeDtypeStruct((t, vw), BF16),
        ],
        compiler_params=pltpu.CompilerParams(dimension_semantics=("arbitrary",), vmem_limit_bytes=VMEM_LIMIT),
        name="inproj",
    )(x2, w_bf, b, cos_t, sin_t)


def _conv_kernel(u_ref, w_ref, b_ref, g_ref, beta_ref, o_ref, buf_ref, sh_ref, *, ts, tiles_per_seq, rb):
    i = pl.program_id(0)

    @pl.when(i % tiles_per_seq == 0)
    def _():
        buf_ref[0:CONV_HALO, :] = jnp.zeros((CONV_HALO, buf_ref.shape[1]), F32)

    @pl.when(i % tiles_per_seq != 0)
    def _():
        buf_ref[0:CONV_HALO, :] = buf_ref[ts:ts + CONV_HALO, :]

    buf_ref[CONV_HALO:CONV_HALO + ts, :] = u_ref[...]

    first = CONV_HALO - (CONV_KERNEL - 1)
    span = ts + CONV_HALO - SUBLANES
    for p in range(1, SUBLANES):
        sh_ref[p - 1, 0:span, :] = buf_ref[p:p + span, :]

    bias = b_ref[...]
    g = g_ref[...]
    beta = beta_ref[...]
    for r0 in range(0, ts, rb):
        acc = jnp.broadcast_to(bias, (rb, bias.shape[1]))
        for j in range(CONV_KERNEL):
            p = (first + j) % SUBLANES
            a0 = r0 + first + j - p
            window = buf_ref[a0:a0 + rb, :] if p == 0 else sh_ref[p - 1, a0:a0 + rb, :]
            acc = acc + w_ref[j:j + 1, :] * window
        y = _layernorm_rows(acc, g, beta)
        o_ref[r0:r0 + rb, :] = (y * jax.nn.sigmoid(y)).astype(o_ref.dtype)


def _conv_module(u, w, b, g, beta, *, seq, ts, rb=32):
    t, c = u.shape
    tiles_per_seq = seq // ts
    return pl.pallas_call(
        functools.partial(_conv_kernel, ts=ts, tiles_per_seq=tiles_per_seq, rb=rb),
        grid=(t // ts,),
        in_specs=[
            pl.BlockSpec((ts, c), lambda i: (i, 0)),
            pl.BlockSpec((CONV_KERNEL, c), lambda i: (0, 0)),
            pl.BlockSpec((1, c), lambda i: (0, 0)),
            pl.BlockSpec((1, c), lambda i: (0, 0)),
            pl.BlockSpec((1, c), lambda i: (0, 0)),
        ],
        out_specs=pl.BlockSpec((ts, c), lambda i: (i, 0)),
        out_shape=jax.ShapeDtypeStruct((t, c), BF16),
        scratch_shapes=[pltpu.VMEM((CONV_HALO + ts, c), F32), pltpu.VMEM((SUBLANES - 1, CONV_HALO + ts, c), F32)],
        compiler_params=pltpu.CompilerParams(dimension_semantics=("arbitrary",)),
        name="conv_module",
    )(u, w, b, g, beta)


def _attn_kernel(lam_ref, g_ref, q_ref, k_ref, v_ref, o_ref, qq_scr, s_scr, p_scr, a_scr, m_scr, l_scr, acc_scr,
                 *, tq, rc, nh, lam_init):
    qi = pl.program_id(2)
    heads = range(nh)
    for h in heads:
        q = q_ref[:, h * VALUE_DIM:(h + 1) * VALUE_DIM]
        lane = lax.broadcasted_iota(jnp.int32, q.shape, 1)
        zero = jnp.zeros_like(q)
        qq_scr[h, 0:tq, :] = jnp.where(lane < HEAD_DIM, q, zero)
        qq_scr[h, tq:, :] = jnp.where(lane >= HEAD_DIM, q, zero)

    m_scr[...] = jnp.full(m_scr.shape, -jnp.inf, F32)
    l_scr[...] = jnp.zeros(l_scr.shape, F32)
    acc_scr[...] = jnp.zeros(acc_scr.shape, F32)

    def kv_rows(j):
        return pl.ds(pl.multiple_of(j * tq, tq), tq)

    def head_cols(h):
        return slice(h * VALUE_DIM, (h + 1) * VALUE_DIM)

    def scores(j, slot):
        for h in heads:
            s_scr[h, slot] = lax.dot_general(qq_scr[h], k_ref[kv_rows(j), head_cols(h)], (((1,), (1,)), ((), ())),
                                             preferred_element_type=F32)

    def softmax(slot, masked):
        for r0 in range(0, 2 * tq, rc):
            for h in heads:
                s = s_scr[h, slot, r0:r0 + rc, :]
                if masked:
                    row = (lax.broadcasted_iota(jnp.int32, s.shape, 0) + r0) % tq
                    col = lax.broadcasted_iota(jnp.int32, s.shape, 1)
                    s = jnp.where((col // CHUNK) <= (row // CHUNK), s, -jnp.inf)
                m_prev = m_scr[h, r0:r0 + rc, :]
                m_new = jnp.maximum(m_prev, jnp.max(s, axis=-1, keepdims=True))
                alpha = jnp.exp2(m_prev - m_new)
                p = jnp.exp2(s - jnp.tile(m_new, (1, tq // LANES)))
                l_scr[h, r0:r0 + rc, :] = alpha * l_scr[h, r0:r0 + rc, :] + jnp.sum(p, axis=-1, keepdims=True)
                m_scr[h, r0:r0 + rc, :] = m_new
                a_scr[h, slot, r0:r0 + rc, :] = alpha
                p_scr[h, slot, r0:r0 + rc, :] = p.astype(BF16)

    def values(j, slot):
        for h in heads:
            acc_scr[h] = a_scr[h, slot] * acc_scr[h] + jnp.dot(p_scr[h, slot], v_ref[kv_rows(j), head_cols(h)],
                                                               preferred_element_type=F32)

    scores(0, 0)

    @pl.when(qi > 0)
    def _():
        scores(1, 1)
        softmax(0, False)

    def body(i, carry):
        for par in (0, 1):
            @pl.when(i % 2 == par)
            def _():
                scores(i + 1, 1 - par)
                softmax(par, False)
                values(i - 1, 1 - par)
        return carry

    lax.fori_loop(1, qi, body, 0)

    for par in (0, 1):
        @pl.when((qi > 0) & (qi % 2 == par))
        def _():
            softmax(par, True)
            values(qi - 1, 1 - par)
            values(qi, par)

    @pl.when(qi == 0)
    def _():
        softmax(0, True)
        values(0, 0)

    lp = lam_ref[...]
    lam = (jnp.exp(jnp.sum(lp[0:1] * lp[1:2], axis=-1, keepdims=True))
           - jnp.exp(jnp.sum(lp[2:3] * lp[3:4], axis=-1, keepdims=True)) + lam_init)
    for h in heads:
        o = acc_scr[h] / l_scr[h]
        o = o[:tq] - lam * o[tq:]
        ms = jnp.mean(o * o, axis=-1, keepdims=True)
        o_ref[:, head_cols(h)] = (o * lax.rsqrt(ms + LN_EPS) * g_ref[...] * (1.0 - lam_init)).astype(o_ref.dtype)


def _diff_attention(q, k, v, lam_params, subln_g, *, batch, seq, tq, lam_init, nh=1):
    t = q.shape[0]
    nq = seq // tq
    w = nh * VALUE_DIM
    return pl.pallas_call(
        functools.partial(_attn_kernel, tq=tq, rc=min(256, tq), nh=nh, lam_init=lam_init),
        grid=(batch, HEADS // nh, nq),
        in_specs=[
            pl.BlockSpec((4, HEAD_DIM), lambda b, h, i: (0, 0)),
            pl.BlockSpec((1, VALUE_DIM), lambda b, h, i: (0, 0)),
            pl.BlockSpec((tq, w), lambda b, h, i: (b * nq + i, h)),
            pl.BlockSpec((seq, w), lambda b, h, i: (b, h)),
            pl.BlockSpec((seq, w), lambda b, h, i: (b, h)),
        ],
        out_specs=pl.BlockSpec((tq, w), lambda b, h, i: (b * nq + i, h)),
        out_shape=jax.ShapeDtypeStruct((t, HEADS * VALUE_DIM), BF16),
        scratch_shapes=[
            pltpu.VMEM((nh, 2 * tq, VALUE_DIM), BF16),
            pltpu.VMEM((nh, 2, 2 * tq, tq), F32),
            pltpu.VMEM((nh, 2, 2 * tq, tq), BF16),
            pltpu.VMEM((nh, 2, 2 * tq, LANES), F32),
            pltpu.VMEM((nh, 2 * tq, LANES), F32),
            pltpu.VMEM((nh, 2 * tq, LANES), F32),
            pltpu.VMEM((nh, 2 * tq, VALUE_DIM), F32),
        ],
        compiler_params=pltpu.CompilerParams(dimension_semantics=("arbitrary", "arbitrary", "arbitrary"),
                                             vmem_limit_bytes=VMEM_LIMIT),
        name="diff_attention",
    )(lam_params, subln_g, q, k, v)


def _outproj_kernel(x_ref, u_ref, o_ref, w_ref, g_ref, b_ref, wr_ref, br_ref,
                    x1_ref, ri_ref, rw_ref, cnt_ref, carry_scr, *, alpha, cw):
    acc = jnp.dot(u_ref[...], w_ref[0:cw, :], preferred_element_type=F32)
    acc = acc + jnp.dot(o_ref[...], w_ref[cw:, :], preferred_element_type=F32)
    x1 = _layernorm_rows(alpha * x_ref[...] + acc, g_ref[...], b_ref[...])
    x1_ref[...] = x1

    x_hi = x1.astype(BF16)
    x_lo = (x1 - x_hi.astype(F32)).astype(BF16)
    hi = jnp.dot(x_hi, wr_ref[...], preferred_element_type=F32)
    logits = (hi[:, :LANES] + hi[:, LANES:] + jnp.dot(x_lo, wr_ref[:, :LANES], preferred_element_type=F32)
              + br_ref[...])
    lane = lax.broadcasted_iota(jnp.int32, logits.shape, 1)
    neg = -jnp.inf
    gmask = lane < N_GROUPS
    lg = jnp.where(gmask, logits, neg)
    gmax = jnp.max(lg, axis=-1, keepdims=True)
    grp = jnp.min(jnp.where(lg == gmax, lane, LANES), axis=-1, keepdims=True)
    gsum = jnp.sum(jnp.where(gmask, jnp.exp(lg - gmax), 0.0), axis=-1, keepdims=True)
    gate = 1.0 / gsum
    emask = (lane >= N_GROUPS) & (lane < N_GROUPS + N_EXPERTS) & (((lane - N_GROUPS) // EXPERTS_PER_GROUP) == grp)
    ev = jnp.where(emask, logits, neg)
    v0 = jnp.max(ev, axis=-1, keepdims=True)
    i0 = jnp.min(jnp.where(emask & (ev == v0), lane, LANES), axis=-1, keepdims=True)
    emask1 = emask & (lane != i0)
    ev1 = jnp.where(emask1, logits, neg)
    v1 = jnp.max(ev1, axis=-1, keepdims=True)
    i1 = jnp.min(jnp.where(emask1 & (ev1 == v1), lane, LANES), axis=-1, keepdims=True)
    e1 = jnp.exp(v1 - v0)
    w0 = gate * (1.0 / (1.0 + e1))
    w1 = gate * (e1 / (1.0 + e1))

    @pl.when(pl.program_id(0) == 0)
    def _():
        carry_scr[...] = jnp.zeros(carry_scr.shape, F32)

    tm = logits.shape[0]
    oh0 = (lane == i0 - N_GROUPS).astype(F32)
    oh1 = (lane == i1 - N_GROUPS).astype(F32)
    both = oh0 + oh1
    tri = (lax.broadcasted_iota(jnp.int32, (tm, tm), 0) > lax.broadcasted_iota(jnp.int32, (tm, tm), 1)).astype(BF16)
    before = jnp.dot(tri, both.astype(BF16), preferred_element_type=F32) + carry_scr[...]
    rank0 = jnp.sum(oh0 * before, axis=-1, keepdims=True).astype(jnp.int32)
    rank1 = jnp.sum(oh1 * before, axis=-1, keepdims=True).astype(jnp.int32)
    carry_scr[...] = carry_scr[...] + jnp.sum(both, axis=0, keepdims=True)
    cnt_ref[...] = carry_scr[...]

    ri_ref[...] = jnp.where(lane == 0, i0 - N_GROUPS, jnp.where(lane == 1, i1 - N_GROUPS,
                            jnp.where(lane == 2, rank0, jnp.where(lane == 3, rank1, 0))))
    rw_ref[...] = jnp.where(lane == 0, w0, jnp.where(lane == 1, w1, 0.0))


def _outproj_router(x2, u2, o, w_bf, g, b, wr, br, *, tm, alpha):
    t, d = x2.shape
    cw = u2.shape[1]
    row = lambda i: (i, 0)
    const = lambda i: (0, 0)
    return pl.pallas_call(
        functools.partial(_outproj_kernel, alpha=alpha, cw=cw),
        grid=(t // tm,),
        in_specs=[
            pl.BlockSpec((tm, d), row),
            pl.BlockSpec((tm, cw), row),
            pl.BlockSpec((tm, o.shape[1]), row),
            pl.BlockSpec(w_bf.shape, const),
            pl.BlockSpec((1, d), const),
            pl.BlockSpec((1, d), const),
            pl.BlockSpec((d, 2 * LANES), const),
            pl.BlockSpec((1, LANES), const),
        ],
        out_specs=[
            pl.BlockSpec((tm, d), row),
            pl.BlockSpec((tm, LANES), row),
            pl.BlockSpec((tm, LANES), row),
            pl.BlockSpec((1, LANES), const),
        ],
        out_shape=[
            jax.ShapeDtypeStruct((t, d), F32),
            jax.ShapeDtypeStruct((t, LANES), jnp.int32),
            jax.ShapeDtypeStruct((t, LANES), F32),
            jax.ShapeDtypeStruct((1, LANES), F32),
        ],
        scratch_shapes=[pltpu.VMEM((1, LANES), F32)],
        compiler_params=pltpu.CompilerParams(dimension_semantics=("arbitrary",), vmem_limit_bytes=VMEM_LIMIT),
        name="outproj_router",
    )(x2, u2, o, w_bf, g, b, wr, br)


def _dispatch_kernel(pos_ref, pend_ref, x_ref, xs_hbm, zbuf, sem, zsem, *, tm):
    i = pl.program_id(0)

    def zero_copy(e):
        start = pl.multiple_of(pend_ref[e + 1] - MOE_BLOCK, MOE_BLOCK)
        return pltpu.make_async_copy(zbuf, xs_hbm.at[pl.ds(start, MOE_BLOCK)], zsem)

    def zero_block(b):
        return pltpu.make_async_copy(zbuf, xs_hbm.at[pl.ds(pl.multiple_of(b * MOE_BLOCK, MOE_BLOCK), MOE_BLOCK)], zsem)

    @pl.when(i == 0)
    def _():
        zbuf[...] = jnp.zeros(zbuf.shape, zbuf.dtype)
        first_unused = pend_ref[N_EXPERTS] // MOE_BLOCK
        n_blocks = xs_hbm.shape[0] // MOE_BLOCK
        for e in range(N_EXPERTS):
            @pl.when(pend_ref[e + 1] > pend_ref[e])
            def _():
                zero_copy(e).start()

        def start_unused(b, c):
            zero_block(b).start()
            return c

        def wait_unused(b, c):
            zero_block(b).wait()
            return c

        lax.fori_loop(first_unused, n_blocks, start_unused, 0)
        for e in range(N_EXPERTS):
            @pl.when(pend_ref[e + 1] > pend_ref[e])
            def _():
                zero_copy(e).wait()
        lax.fori_loop(first_unused, n_blocks, wait_unused, 0)

    def row_copy(r, dst_row):
        return pltpu.make_async_copy(x_ref.at[pl.ds(r % tm, 1)], xs_hbm.at[pl.ds(dst_row, 1)], sem)

    for r in range(2 * tm):
        row_copy(r, pos_ref[i * 2 * tm + r]).start()
    for r in range(2 * tm):
        row_copy(r, 0).wait()


def _dispatch(pos, pad_ends, x1, *, n_rows, tm):
    t, d = x1.shape
    grid_spec = pltpu.PrefetchScalarGridSpec(
        num_scalar_prefetch=2,
        grid=(t // tm,),
        in_specs=[pl.BlockSpec((tm, d), lambda i, pos, pe: (i, 0))],
        out_specs=pl.BlockSpec(memory_space=pl.ANY),
        scratch_shapes=[pltpu.VMEM((MOE_BLOCK, d), x1.dtype), pltpu.SemaphoreType.DMA(()), pltpu.SemaphoreType.DMA(())],
    )
    return pl.pallas_call(
        functools.partial(_dispatch_kernel, tm=tm),
        grid_spec=grid_spec,
        out_shape=jax.ShapeDtypeStruct((n_rows, d), x1.dtype),
        compiler_params=pltpu.CompilerParams(dimension_semantics=("arbitrary",)),
        name="moe_dispatch",
    )(pos, pad_ends, x1)


def _expert_kernel(blk_e_ref, nused_ref, xs_ref, wg_ref, wu_ref, wd_ref, y_ref, wg_bf, wu_bf, wd_bf):
    i = pl.program_id(0)

    @pl.when(i < nused_ref[0])
    def _():
        @pl.when((i == 0) | (blk_e_ref[i] != blk_e_ref[jnp.maximum(i - 1, 0)]))
        def _():
            wg_bf[...] = wg_ref[...].astype(BF16)
            wu_bf[...] = wu_ref[...].astype(BF16)
            wd_bf[...] = wd_ref[...].astype(BF16)

        xb = xs_ref[...].astype(BF16)
        gate = jnp.dot(xb, wg_bf[...], preferred_element_type=F32)
        up = jnp.dot(xb, wu_bf[...], preferred_element_type=F32)
        h = (gate * jax.nn.sigmoid(gate)) * up
        y_ref[...] = jnp.dot(h.astype(BF16), wd_bf[...], preferred_element_type=F32)

    @pl.when(i >= nused_ref[0])
    def _():
        y_ref[...] = jnp.zeros(y_ref.shape, y_ref.dtype)


def _expert_mlp(blk_e, n_used, xs, wg, wu, wd, *, layer):
    n_rows, d = xs.shape
    n_blocks = n_rows // MOE_BLOCK
    de = wg.shape[3]
    live = lambda i, be, nu: jnp.maximum(jnp.minimum(i, nu[0] - 1), 0)
    grid_spec = pltpu.PrefetchScalarGridSpec(
        num_scalar_prefetch=2,
        grid=(n_blocks,),
        in_specs=[
            pl.BlockSpec((MOE_BLOCK, d), lambda i, be, nu: (live(i, be, nu), 0)),
            pl.BlockSpec((None, None, d, de), lambda i, be, nu: (layer, be[live(i, be, nu)], 0, 0)),
            pl.BlockSpec((None, None, d, de), lambda i, be, nu: (layer, be[live(i, be, nu)], 0, 0)),
            pl.BlockSpec((None, None, de, d), lambda i, be, nu: (layer, be[live(i, be, nu)], 0, 0)),
        ],
        out_specs=pl.BlockSpec((MOE_BLOCK, d), lambda i, be, nu: (i, 0)),
        scratch_shapes=[pltpu.VMEM((d, de), BF16), pltpu.VMEM((d, de), BF16), pltpu.VMEM((de, d), BF16)],
    )
    return pl.pallas_call(
        _expert_kernel,
        grid_spec=grid_spec,
        out_shape=jax.ShapeDtypeStruct((n_rows, d), F32),
        compiler_params=pltpu.CompilerParams(dimension_semantics=("arbitrary",), vmem_limit_bytes=VMEM_LIMIT),
        name="expert_mlp",
    )(blk_e, n_used, xs, wg, wu, wd)


def _row_gather_copy(src_hbm, dst_buf, sem, slot, row, src_row):
    return pltpu.make_async_copy(src_hbm.at[pl.ds(src_row, 1)], dst_buf.at[slot, pl.ds(row, 1)], sem.at[slot])


def _combine_kernel(pos_ref, y_hbm, x_ref, rw_ref, g_ref, b_ref, o_ref, ybuf, sem, *, tm, alpha):
    i = pl.program_id(0)
    n = pl.num_programs(0)
    slot = i % 2

    def issue(tile, s):
        for r in range(2 * tm):
            _row_gather_copy(y_hbm, ybuf, sem, s, r, pos_ref[tile * 2 * tm + r]).start()

    @pl.when(i == 0)
    def _():
        issue(0, 0)

    @pl.when(i + 1 < n)
    def _():
        issue(i + 1, 1 - slot)

    for r in range(2 * tm):
        _row_gather_copy(y_hbm, ybuf, sem, slot, r, 0).wait()

    yy = ybuf[slot]
    rw = rw_ref[...]
    f = yy[:tm] * rw[:, 0:1] + yy[tm:] * rw[:, 1:2]
    o_ref[...] = _layernorm_rows(alpha * x_ref[...] + f, g_ref[...], b_ref[...])


def _combine(pos, y, x1, route_w, g, b, *, tm, alpha):
    t, d = x1.shape
    grid_spec = pltpu.PrefetchScalarGridSpec(
        num_scalar_prefetch=1,
        grid=(t // tm,),
        in_specs=[
            pl.BlockSpec(memory_space=pl.ANY),
            pl.BlockSpec((tm, d), lambda i, pos: (i, 0)),
            pl.BlockSpec((tm, LANES), lambda i, pos: (i, 0)),
            pl.BlockSpec((1, d), lambda i, pos: (0, 0)),
            pl.BlockSpec((1, d), lambda i, pos: (0, 0)),
        ],
        out_specs=pl.BlockSpec((tm, d), lambda i, pos: (i, 0)),
        scratch_shapes=[pltpu.VMEM((2, 2 * tm, d), F32), pltpu.SemaphoreType.DMA((2,))],
    )
    return pl.pallas_call(
        functools.partial(_combine_kernel, tm=tm, alpha=alpha),
        grid_spec=grid_spec,
        out_shape=jax.ShapeDtypeStruct((t, d), F32),
        compiler_params=pltpu.CompilerParams(dimension_semantics=("arbitrary",), vmem_limit_bytes=VMEM_LIMIT),
        name="moe_combine",
    )(pos, y, x1, route_w, g, b)


def _dispatch_plan(route_i, counts, *, tm):
    t = route_i.shape[0]
    counts = counts[0, :N_EXPERTS].astype(jnp.int32)
    padded = ((counts + MOE_BLOCK - 1) // MOE_BLOCK) * MOE_BLOCK
    pad_end = jnp.cumsum(padded)
    pad_start = pad_end - padded
    dest = pad_start[route_i[:, 0:2]] + route_i[:, 2:4]
    pos = dest.reshape(t // tm, tm, 2).transpose(0, 2, 1).reshape(-1).astype(jnp.int32)
    n_blocks = -(-(2 * t) // MOE_BLOCK) + N_EXPERTS
    block_start = jnp.arange(n_blocks, dtype=jnp.int32) * MOE_BLOCK
    blk_e = jnp.minimum(jnp.sum((pad_end[None, :] <= block_start[:, None]).astype(jnp.int32), axis=1), N_EXPERTS - 1)
    pad_ends = jnp.concatenate([jnp.zeros((1,), jnp.int32), pad_end.astype(jnp.int32)])
    n_used = (pad_end[-1:] // MOE_BLOCK).astype(jnp.int32)
    return pos, pad_ends, blk_e, n_used, n_blocks * MOE_BLOCK


def _rope_tables(seq):
    half = HEAD_DIM // 2
    inv_freq = 1.0 / (ROPE_THETA ** (jnp.arange(half, dtype=F32) * 2.0 / HEAD_DIM))
    ang = jnp.arange(seq, dtype=F32)[:, None] * inv_freq[None, :]
    cos, sin = jnp.cos(ang), jnp.sin(ang)
    reps = LANES // HEAD_DIM
    return jnp.tile(jnp.concatenate([cos, cos], axis=1), (1, reps)), jnp.tile(jnp.concatenate([-sin, sin], axis=1), (1, reps))


def _pick_tile(n, want):
    while n % want:
        want //= 2
    return want


def kernel(x, w_in, b_in, conv_w, conv_b, conv_ln_g, conv_ln_b, lam_q1, lam_k1, lam_q2, lam_k2, subln_g, w_out, ln1_g, ln1_b, w_rg, b_rg, w_re, b_re, w_gate_e, w_up_e, w_down_e, ln2_g, ln2_b):
    batch, seq, d = x.shape
    depth = w_in.shape[0]
    t = batch * seq
    alpha = (2.0 * depth) ** 0.25
    cos_t, sin_t = _rope_tables(seq)
    tm = _pick_tile(seq, 512)
    ts = _pick_tile(seq, 256)
    tq = _pick_tile(seq, 512)
    tc = _pick_tile(seq, 256)
    row = lambda a: a.reshape(1, -1)

    x2 = x.reshape(t, d)
    for l in range(depth):
        lam_init = 0.8 - 0.6 * math.exp(-0.3 * l)
        u, q, k, v = _inproj(x2, w_in[l].astype(BF16), row(b_in[l]), cos_t, sin_t, seq=seq, tm=tm)
        u2 = _conv_module(u, conv_w[l].reshape(CONV_KERNEL, -1), row(conv_b[l]), row(conv_ln_g[l]),
                          row(conv_ln_b[l]), seq=seq, ts=ts)
        lam_params = jnp.stack([lam_q1[l], lam_k1[l], lam_q2[l], lam_k2[l]])
        o = _diff_attention(q, k, v, lam_params, row(subln_g[l]), batch=batch, seq=seq, tq=tq, lam_init=lam_init)
        w_router = jnp.concatenate([w_rg[l], w_re[l]], axis=1)
        w_router = jnp.pad(w_router, ((0, 0), (0, LANES - w_router.shape[1])))
        w_router_hi = w_router.astype(BF16)
        w_router = jnp.concatenate([w_router_hi, (w_router - w_router_hi.astype(F32)).astype(BF16)], axis=1)
        b_router = jnp.pad(jnp.concatenate([b_rg[l], b_re[l]]), (0, LANES - N_GROUPS - N_EXPERTS))
        x1, route_i, route_w, counts = _outproj_router(x2, u2, o, w_out[l].astype(BF16), row(ln1_g[l]), row(ln1_b[l]),
                                                       w_router, row(b_router), tm=tm, alpha=alpha)
        pos, pad_ends, blk_e, n_used, n_rows = _dispatch_plan(route_i, counts, tm=tc)
        xs = _dispatch(pos, pad_ends, x1, n_rows=n_rows, tm=tc)
        y = _expert_mlp(blk_e, n_used, xs, w_gate_e, w_up_e, w_down_e, layer=l)
        x2 = _combine(pos, y, x1, route_w, row(ln2_g[l]), row(ln2_b[l]), tm=tc, alpha=alpha)
    return x2.reshape(batch, seq, d)
```

```python
import functools
import math

import jax
import jax.numpy as jnp
from jax import lax
from jax.experimental import pallas as pl
from jax.experimental.pallas import tpu as pltpu

F32 = jnp.float32
BF16 = jnp.bfloat16

CHUNK = 64
CONV_KERNEL = 31
HEADS = 4
HEAD_DIM = 64
VALUE_DIM = 2 * HEAD_DIM
ROPE_THETA = 10000.0
N_GROUPS = 4
EXPERTS_PER_GROUP = 8
N_EXPERTS = N_GROUPS * EXPERTS_PER_GROUP
MOE_BLOCK = 256
LN_EPS = 1e-5
LANES = 128
SUBLANES = 8
CONV_HALO = 32
VMEM_LIMIT = 56 * 1024 * 1024


def _layernorm_rows(y, g, b):
    mu = jnp.mean(y, axis=-1, keepdims=True)
    d = y - mu
    var = jnp.mean(d * d, axis=-1, keepdims=True)
    return d * lax.rsqrt(var + LN_EPS) * g + b


def _inproj_kernel(x_ref, w_ref, b_ref, cos_ref, sin_ref, u_ref, q_ref, k_ref, v_ref, *, cw, qk):
    xb = x_ref[...].astype(BF16)

    def proj(c0, n):
        return jnp.dot(xb, w_ref[:, c0:c0 + n], preferred_element_type=F32) + b_ref[:, c0:c0 + n]

    ag = proj(0, 2 * cw)
    u_ref[...] = ag[:, :cw] * jax.nn.sigmoid(ag[:, cw:])

    cos = cos_ref[...]
    sin = sin_ref[...]
    lane = lax.broadcasted_iota(jnp.int32, cos.shape, 1)
    first_half = (lane % HEAD_DIM) < (HEAD_DIM // 2)

    def rope(z, scale):
        outs = []
        for c in range(0, qk, LANES):
            zz = z[:, c:c + LANES]
            swapped = jnp.where(first_half, pltpu.roll(zz, LANES - HEAD_DIM // 2, 1),
                                pltpu.roll(zz, HEAD_DIM // 2, 1))
            outs.append((zz * cos + swapped * sin) * scale)
        return jnp.concatenate(outs, axis=1)

    q_ref[...] = rope(proj(2 * cw, qk), HEAD_DIM ** -0.5 * math.log2(math.e)).astype(BF16)
    k_ref[...] = rope(proj(2 * cw + qk, qk), 1.0).astype(BF16)
    v_ref[...] = proj(2 * cw + 2 * qk, v_ref.shape[1]).astype(BF16)


def _inproj(x2, w_bf, b, cos_t, sin_t, *, seq, tm):
    t, d = x2.shape
    n = w_bf.shape[1]
    cw = d // 2
    qk = 2 * HEADS * HEAD_DIM
    vw = HEADS * VALUE_DIM
    tiles_per_seq = seq // tm
    return pl.pallas_call(
        functools.partial(_inproj_kernel, cw=cw, qk=qk),
        grid=(t // tm,),
        in_specs=[
            pl.BlockSpec((tm, d), lambda i: (i, 0)),
            pl.BlockSpec((d, n), lambda i: (0, 0)),
            pl.BlockSpec((1, n), lambda i: (0, 0)),
            pl.BlockSpec((tm, LANES), lambda i: (i % tiles_per_seq, 0)),
            pl.BlockSpec((tm, LANES), lambda i: (i % tiles_per_seq, 0)),
        ],
        out_specs=[
            pl.BlockSpec((tm, cw), lambda i: (i, 0)),
            pl.BlockSpec((tm, qk), lambda i: (i, 0)),
            pl.BlockSpec((tm, qk), lambda i: (i, 0)),
            pl.BlockSpec((tm, vw), lambda i: (i, 0)),
        ],
        out_shape=[
            jax.ShapeDtypeStruct((t, cw), F32),
            jax.ShapeDtypeStruct((t, qk), BF16),
            jax.ShapeDtypeStruct((t, qk), BF16),
            jax.ShapeDtypeStruct((t, vw), BF16),
        ],
        compiler_params=pltpu.CompilerParams(dimension_semantics=("arbitrary",), vmem_limit_bytes=VMEM_LIMIT),
        name="inproj",
    )(x2, w_bf, b, cos_t, sin_t)


def _conv_kernel(u_ref, w_ref, b_ref, g_ref, beta_ref, o_ref, buf_ref, sh_ref, *, ts, tiles_per_seq, rb):
    i = pl.program_id(0)

    @pl.when(i % tiles_per_seq == 0)
    def _():
        buf_ref[0:CONV_HALO, :] = jnp.zeros((CONV_HALO, buf_ref.shape[1]), F32)

    @pl.when(i % tiles_per_seq != 0)
    def _():
        buf_ref[0:CONV_HALO, :] = buf_ref[ts:ts + CONV_HALO, :]

    buf_ref[CONV_HALO:CONV_HALO + ts, :] = u_ref[...]

    first = CONV_HALO - (CONV_KERNEL - 1)
    span = ts + CONV_HALO - SUBLANES
    for p in range(1, SUBLANES):
        sh_ref[p - 1, 0:span, :] = buf_ref[p:p + span, :]

    bias = b_ref[...]
    g = g_ref[...]
    beta = beta_ref[...]
    for r0 in range(0, ts, rb):
        acc = jnp.broadcast_to(bias, (rb, bias.shape[1]))
        for j in range(CONV_KERNEL):
            p = (first + j) % SUBLANES
            a0 = r0 + first + j - p
            window = buf_ref[a0:a0 + rb, :] if p == 0 else sh_ref[p - 1, a0:a0 + rb, :]
            acc = acc + w_ref[j:j + 1, :] * window
        y = _layernorm_rows(acc, g, beta)
        o_ref[r0:r0 + rb, :] = (y * jax.nn.sigmoid(y)).astype(o_ref.dtype)


def _conv_module(u, w, b, g, beta, *, seq, ts, rb=32):
    t, c = u.shape
    tiles_per_seq = seq // ts
    return pl.pallas_call(
        functools.partial(_conv_kernel, ts=ts, tiles_per_seq=tiles_per_seq, rb=rb),
        grid=(t // ts,),
        in_specs=[
            pl.BlockSpec((ts, c), lambda i: (i, 0)),
            pl.BlockSpec((CONV_KERNEL, c), lambda i: (0, 0)),
            pl.BlockSpec((1, c), lambda i: (0, 0)),
            pl.BlockSpec((1, c), lambda i: (0, 0)),
            pl.BlockSpec((1, c), lambda i: (0, 0)),
        ],
        out_specs=pl.BlockSpec((ts, c), lambda i: (i, 0)),
        out_shape=jax.ShapeDtypeStruct((t, c), BF16),
        scratch_shapes=[pltpu.VMEM((CONV_HALO + ts, c), F32), pltpu.VMEM((SUBLANES - 1, CONV_HALO + ts, c), F32)],
        compiler_params=pltpu.CompilerParams(dimension_semantics=("arbitrary",)),
        name="conv_module",
    )(u, w, b, g, beta)


def _attn_kernel(lam_ref, g_ref, q_ref, k_ref, v_ref, o_ref, qq_scr, s_scr, p_scr, a_scr, m_scr, l_scr, acc_scr,
                 *, tq, rc, nh, lam_init):
    qi = pl.program_id(2)
    heads = range(nh)
    for h in heads:
        q = q_ref[:, h * VALUE_DIM:(h + 1) * VALUE_DIM]
        lane = lax.broadcasted_iota(jnp.int32, q.shape, 1)
        zero = jnp.zeros_like(q)
        qq_scr[h, 0:tq, :] = jnp.where(lane < HEAD_DIM, q, zero)
        qq_scr[h, tq:, :] = jnp.where(lane >= HEAD_DIM, q, zero)

    m_scr[...] = jnp.full(m_scr.shape, -jnp.inf, F32)
    l_scr[...] = jnp.zeros(l_scr.shape, F32)
    acc_scr[...] = jnp.zeros(acc_scr.shape, F32)

    def kv_rows(j):
        return pl.ds(pl.multiple_of(j * tq, tq), tq)

    def head_cols(h):
        return slice(h * VALUE_DIM, (h + 1) * VALUE_DIM)

    def scores(j, slot):
        for h in heads:
            s_scr[h, slot] = lax.dot_general(qq_scr[h], k_ref[kv_rows(j), head_cols(h)], (((1,), (1,)), ((), ())),
                                             preferred_element_type=F32)

    def softmax(slot, masked):
        for r0 in range(0, 2 * tq, rc):
            for h in heads:
                s = s_scr[h, slot, r0:r0 + rc, :]
                if masked:
                    row = (lax.broadcasted_iota(jnp.int32, s.shape, 0) + r0) % tq
                    col = lax.broadcasted_iota(jnp.int32, s.shape, 1)
                    s = jnp.where((col // CHUNK) <= (row // CHUNK), s, -jnp.inf)
                m_prev = m_scr[h, r0:r0 + rc, :]
                m_new = jnp.maximum(m_prev, jnp.max(s, axis=-1, keepdims=True))
                alpha = jnp.exp2(m_prev - m_new)
                p = jnp.exp2(s - jnp.tile(m_new, (1, tq // LANES)))
                l_scr[h, r0:r0 + rc, :] = alpha * l_scr[h, r0:r0 + rc, :] + jnp.sum(p, axis=-1, keepdims=True)
                m_scr[h, r0:r0 + rc, :] = m_new
                a_scr[h, slot, r0:r0 + rc, :] = alpha
                p_scr[h, slot, r0:r0 + rc, :] = p.astype(BF16)

    def values(j, slot):
        for h in heads:
            acc_scr[h] = a_scr[h, slot] * acc_scr[h] + jnp.dot(p_scr[h, slot], v_ref[kv_rows(j), head_cols(h)],
                                                               preferred_element_type=F32)

    def step(i, par):
        scores(i + 1, 1 - par)
        softmax(par, False)
        values(i - 1, 1 - par)

    def finish(par):
        softmax(par, True)
        values(qi - 1, 1 - par)
        values(qi, par)

    @pl.when(qi == 0)
    def _():
        scores(0, 0)
        softmax(0, True)
        values(0, 0)

    @pl.when(qi > 0)
    def _():
        scores(0, 0)
        scores(1, 1)
        softmax(0, False)

    def body(i, carry):
        for par in (0, 1):
            @pl.when(i % 2 == par)
            def _():
                step(i, par)
        return carry

    lax.fori_loop(1, qi, body, 0)

    for par in (0, 1):
        @pl.when((qi > 0) & (qi % 2 == par))
        def _():
            finish(par)

    lp = lam_ref[...]
    lam = (jnp.exp(jnp.sum(lp[0:1] * lp[1:2], axis=-1, keepdims=True))
           - jnp.exp(jnp.sum(lp[2:3] * lp[3:4], axis=-1, keepdims=True)) + lam_init)
    for h in heads:
        o = acc_scr[h] / l_scr[h]
        o = o[:tq] - lam * o[tq:]
        ms = jnp.mean(o * o, axis=-1, keepdims=True)
        o_ref[:, head_cols(h)] = (o * lax.rsqrt(ms + LN_EPS) * g_ref[...] * (1.0 - lam_init)).astype(o_ref.dtype)


def _diff_attention(q, k, v, lam_params, subln_g, *, batch, seq, tq, lam_init, nh=1):
    t = q.shape[0]
    nq = seq // tq
    w = nh * VALUE_DIM
    return pl.pallas_call(
        functools.partial(_attn_kernel, tq=tq, rc=min(256, tq), nh=nh, lam_init=lam_init),
        grid=(batch, HEADS // nh, nq),
        in_specs=[
            pl.BlockSpec((4, HEAD_DIM), lambda b, h, i: (0, 0)),
            pl.BlockSpec((1, VALUE_DIM), lambda b, h, i: (0, 0)),
            pl.BlockSpec((tq, w), lambda b, h, i: (b * nq + i, h)),
            pl.BlockSpec((seq, w), lambda b, h, i: (b, h)),
            pl.BlockSpec((seq, w), lambda b, h, i: (b, h)),
        ],
        out_specs=pl.BlockSpec((tq, w), lambda b, h, i: (b * nq + i, h)),
        out_shape=jax.ShapeDtypeStruct((t, HEADS * VALUE_DIM), BF16),
        scratch_shapes=[
            pltpu.VMEM((nh, 2 * tq, VALUE_DIM), BF16),
            pltpu.VMEM((nh, 2, 2 * tq, tq), F32),
            pltpu.VMEM((nh, 2, 2 * tq, tq), BF16),
            pltpu.VMEM((nh, 2, 2 * tq, LANES), F32),
            pltpu.VMEM((nh, 2 * tq, LANES), F32),
            pltpu.VMEM((nh, 2 * tq, LANES), F32),
            pltpu.VMEM((nh, 2 * tq, VALUE_DIM), F32),
        ],
        compiler_params=pltpu.CompilerParams(dimension_semantics=("arbitrary", "arbitrary", "arbitrary"),
                                             vmem_limit_bytes=VMEM_LIMIT),
        name="diff_attention",
    )(lam_params, subln_g, q, k, v)


def _outproj_kernel(x_ref, u_ref, o_ref, w_ref, g_ref, b_ref, wr_ref, br_ref,
                    x1_ref, ri_ref, rw_ref, cnt_ref, carry_scr, *, alpha, cw):
    acc = jnp.dot(u_ref[...], w_ref[0:cw, :], preferred_element_type=F32)
    acc = acc + jnp.dot(o_ref[...], w_ref[cw:, :], preferred_element_type=F32)
    x1 = _layernorm_rows(alpha * x_ref[...] + acc, g_ref[...], b_ref[...])
    x1_ref[...] = x1

    x_hi = x1.astype(BF16)
    x_lo = (x1 - x_hi.astype(F32)).astype(BF16)
    hi = jnp.dot(x_hi, wr_ref[...], preferred_element_type=F32)
    logits = (hi[:, :LANES] + hi[:, LANES:] + jnp.dot(x_lo, wr_ref[:, :LANES], preferred_element_type=F32)
              + br_ref[...])
    lane = lax.broadcasted_iota(jnp.int32, logits.shape, 1)
    neg = -jnp.inf
    gmask = lane < N_GROUPS
    lg = jnp.where(gmask, logits, neg)
    gmax = jnp.max(lg, axis=-1, keepdims=True)
    grp = jnp.min(jnp.where(lg == gmax, lane, LANES), axis=-1, keepdims=True)
    gsum = jnp.sum(jnp.where(gmask, jnp.exp(lg - gmax), 0.0), axis=-1, keepdims=True)
    gate = 1.0 / gsum
    emask = (lane >= N_GROUPS) & (lane < N_GROUPS + N_EXPERTS) & (((lane - N_GROUPS) // EXPERTS_PER_GROUP) == grp)
    ev = jnp.where(emask, logits, neg)
    v0 = jnp.max(ev, axis=-1, keepdims=True)
    i0 = jnp.min(jnp.where(emask & (ev == v0), lane, LANES), axis=-1, keepdims=True)
    emask1 = emask & (lane != i0)
    ev1 = jnp.where(emask1, logits, neg)
    v1 = jnp.max(ev1, axis=-1, keepdims=True)
    i1 = jnp.min(jnp.where(emask1 & (ev1 == v1), lane, LANES), axis=-1, keepdims=True)
    e1 = jnp.exp(v1 - v0)
    w0 = gate * (1.0 / (1.0 + e1))
    w1 = gate * (e1 / (1.0 + e1))

    @pl.when(pl.program_id(0) == 0)
    def _():
        carry_scr[...] = jnp.zeros(carry_scr.shape, F32)

    tm = logits.shape[0]
    oh0 = (lane == i0 - N_GROUPS).astype(F32)
    oh1 = (lane == i1 - N_GROUPS).astype(F32)
    both = oh0 + oh1
    tri = (lax.broadcasted_iota(jnp.int32, (tm, tm), 0) > lax.broadcasted_iota(jnp.int32, (tm, tm), 1)).astype(BF16)
    before = jnp.dot(tri, both.astype(BF16), preferred_element_type=F32) + carry_scr[...]
    rank0 = jnp.sum(oh0 * before, axis=-1, keepdims=True).astype(jnp.int32)
    rank1 = jnp.sum(oh1 * before, axis=-1, keepdims=True).astype(jnp.int32)
    carry_scr[...] = carry_scr[...] + jnp.sum(both, axis=0, keepdims=True)
    cnt_ref[...] = carry_scr[...]

    ri_ref[...] = jnp.where(lane == 0, i0 - N_GROUPS, jnp.where(lane == 1, i1 - N_GROUPS,
                            jnp.where(lane == 2, rank0, jnp.where(lane == 3, rank1, 0))))
    rw_ref[...] = jnp.where(lane == 0, w0, jnp.where(lane == 1, w1, 0.0))


def _outproj_router(x2, u2, o, w_bf, g, b, wr, br, *, tm, alpha):
    t, d = x2.shape
    cw = u2.shape[1]
    row = lambda i: (i, 0)
    const = lambda i: (0, 0)
    return pl.pallas_call(
        functools.partial(_outproj_kernel, alpha=alpha, cw=cw),
        grid=(t // tm,),
        in_specs=[
            pl.BlockSpec((tm, d), row),
            pl.BlockSpec((tm, cw), row),
            pl.BlockSpec((tm, o.shape[1]), row),
            pl.BlockSpec(w_bf.shape, const),
            pl.BlockSpec((1, d), const),
            pl.BlockSpec((1, d), const),
            pl.BlockSpec((d, 2 * LANES), const),
            pl.BlockSpec((1, LANES), const),
        ],
        out_specs=[
            pl.BlockSpec((tm, d), row),
            pl.BlockSpec((tm, LANES), row),
            pl.BlockSpec((tm, LANES), row),
            pl.BlockSpec((1, LANES), const),
        ],
        out_shape=[
            jax.ShapeDtypeStruct((t, d), F32),
            jax.ShapeDtypeStruct((t, LANES), jnp.int32),
            jax.ShapeDtypeStruct((t, LANES), F32),
            jax.ShapeDtypeStruct((1, LANES), F32),
        ],
        scratch_shapes=[pltpu.VMEM((1, LANES), F32)],
        compiler_params=pltpu.CompilerParams(dimension_semantics=("arbitrary",), vmem_limit_bytes=VMEM_LIMIT),
        name="outproj_router",
    )(x2, u2, o, w_bf, g, b, wr, br)


def _dispatch_kernel(pos_ref, pend_ref, x_ref, xs_hbm, zbuf, sem, zsem, *, tm):
    i = pl.program_id(0)

    def zero_copy(e):
        start = pl.multiple_of(pend_ref[e + 1] - MOE_BLOCK, MOE_BLOCK)
        return pltpu.make_async_copy(zbuf, xs_hbm.at[pl.ds(start, MOE_BLOCK)], zsem)

    def zero_block(b):
        return pltpu.make_async_copy(zbuf, xs_hbm.at[pl.ds(pl.multiple_of(b * MOE_BLOCK, MOE_BLOCK), MOE_BLOCK)], zsem)

    @pl.when(i == 0)
    def _():
        zbuf[...] = jnp.zeros(zbuf.shape, zbuf.dtype)
        first_unused = pend_ref[N_EXPERTS] // MOE_BLOCK
        n_blocks = xs_hbm.shape[0] // MOE_BLOCK
        for e in range(N_EXPERTS):
            @pl.when(pend_ref[e + 1] > pend_ref[e])
            def _():
                zero_copy(e).start()

        def start_unused(b, c):
            zero_block(b).start()
            return c

        def wait_unused(b, c):
            zero_block(b).wait()
            return c

        lax.fori_loop(first_unused, n_blocks, start_unused, 0)
        for e in range(N_EXPERTS):
            @pl.when(pend_ref[e + 1] > pend_ref[e])
            def _():
                zero_copy(e).wait()
        lax.fori_loop(first_unused, n_blocks, wait_unused, 0)

    def row_copy(r, dst_row):
        return pltpu.make_async_copy(x_ref.at[pl.ds(r % tm, 1)], xs_hbm.at[pl.ds(dst_row, 1)], sem)

    for r in range(2 * tm):
        row_copy(r, pos_ref[i * 2 * tm + r]).start(priority=r % 2)
    for r in range(2 * tm):
        row_copy(r, 0).wait()


def _dispatch(pos, pad_ends, x1, *, n_rows, tm):
    t, d = x1.shape
    grid_spec = pltpu.PrefetchScalarGridSpec(
        num_scalar_prefetch=2,
        grid=(t // tm,),
        in_specs=[pl.BlockSpec((tm, d), lambda i, pos, pe: (i, 0))],
        out_specs=pl.BlockSpec(memory_space=pl.ANY),
        scratch_shapes=[pltpu.VMEM((MOE_BLOCK, d), x1.dtype), pltpu.SemaphoreType.DMA(()), pltpu.SemaphoreType.DMA(())],
    )
    return pl.pallas_call(
        functools.partial(_dispatch_kernel, tm=tm),
        grid_spec=grid_spec,
        out_shape=jax.ShapeDtypeStruct((n_rows, d), x1.dtype),
        compiler_params=pltpu.CompilerParams(dimension_semantics=("arbitrary",)),
        name="moe_dispatch",
    )(pos, pad_ends, x1)


def _expert_kernel(blk_e_ref, nused_ref, xs_ref, wg_ref, wu_ref, wd_ref, y_ref, wg_bf, wu_bf, wd_bf):
    i = pl.program_id(0)

    @pl.when(i < nused_ref[0])
    def _():
        @pl.when((i == 0) | (blk_e_ref[i] != blk_e_ref[jnp.maximum(i - 1, 0)]))
        def _():
            wg_bf[...] = wg_ref[...].astype(BF16)
            wu_bf[...] = wu_ref[...].astype(BF16)
            wd_bf[...] = wd_ref[...].astype(BF16)

        xb = xs_ref[...].astype(BF16)
        gate = jnp.dot(xb, wg_bf[...], preferred_element_type=F32)
        up = jnp.dot(xb, wu_bf[...], preferred_element_type=F32)
        h = (gate * jax.nn.sigmoid(gate)) * up
        y_ref[...] = jnp.dot(h.astype(BF16), wd_bf[...], preferred_element_type=F32)

    @pl.when(i >= nused_ref[0])
    def _():
        y_ref[...] = jnp.zeros(y_ref.shape, y_ref.dtype)


def _expert_mlp(blk_e, n_used, xs, wg, wu, wd, *, layer):
    n_rows, d = xs.shape
    n_blocks = n_rows // MOE_BLOCK
    de = wg.shape[3]
    live = lambda i, be, nu: jnp.maximum(jnp.minimum(i, nu[0] - 1), 0)
    grid_spec = pltpu.PrefetchScalarGridSpec(
        num_scalar_prefetch=2,
        grid=(n_blocks,),
        in_specs=[
            pl.BlockSpec((MOE_BLOCK, d), lambda i, be, nu: (live(i, be, nu), 0)),
            pl.BlockSpec((None, None, d, de), lambda i, be, nu: (layer, be[live(i, be, nu)], 0, 0)),
            pl.BlockSpec((None, None, d, de), lambda i, be, nu: (layer, be[live(i, be, nu)], 0, 0)),
            pl.BlockSpec((None, None, de, d), lambda i, be, nu: (layer, be[live(i, be, nu)], 0, 0)),
        ],
        out_specs=pl.BlockSpec((MOE_BLOCK, d), lambda i, be, nu: (i, 0)),
        scratch_shapes=[pltpu.VMEM((d, de), BF16), pltpu.VMEM((d, de), BF16), pltpu.VMEM((de, d), BF16)],
    )
    return pl.pallas_call(
        _expert_kernel,
        grid_spec=grid_spec,
        out_shape=jax.ShapeDtypeStruct((n_rows, d), F32),
        compiler_params=pltpu.CompilerParams(dimension_semantics=("arbitrary",), vmem_limit_bytes=VMEM_LIMIT),
        name="expert_mlp",
    )(blk_e, n_used, xs, wg, wu, wd)


def _row_gather_copy(src_hbm, dst_buf, sem, slot, row, src_row):
    return pltpu.make_async_copy(src_hbm.at[pl.ds(src_row, 1)], dst_buf.at[slot, pl.ds(row, 1)], sem.at[slot])


def _combine_kernel(pos_ref, y_hbm, x_ref, rw_ref, g_ref, b_ref, o_ref, ybuf, sem, *, tm, alpha):
    i = pl.program_id(0)
    n = pl.num_programs(0)
    slot = i % 2

    def issue(tile, s):
        for r in range(2 * tm):
            _row_gather_copy(y_hbm, ybuf, sem, s, r, pos_ref[tile * 2 * tm + r]).start(priority=r % 2)

    def drain(s):
        for r in range(2 * tm):
            _row_gather_copy(y_hbm, ybuf, sem, s, r, 0).wait()

    @pl.when(i == 0)
    def _():
        issue(0, 0)
        issue(jnp.minimum(1, n - 1), 1)

    drain(slot)
    yy = ybuf[slot]
    rw = rw_ref[...]
    f = yy[:tm] * rw[:, 0:1] + yy[tm:] * rw[:, 1:2]
    o_ref[...] = _layernorm_rows(alpha * x_ref[...] + f, g_ref[...], b_ref[...])
    issue(jnp.minimum(i + 2, n - 1), slot)

    @pl.when(i == n - 1)
    def _():
        drain(slot)
        drain(1 - slot)


def _combine(pos, y, x1, route_w, g, b, *, tm, alpha):
    t, d = x1.shape
    grid_spec = pltpu.PrefetchScalarGridSpec(
        num_scalar_prefetch=1,
        grid=(t // tm,),
        in_specs=[
            pl.BlockSpec(memory_space=pl.ANY),
            pl.BlockSpec((tm, d), lambda i, pos: (i, 0)),
            pl.BlockSpec((tm, LANES), lambda i, pos: (i, 0)),
            pl.BlockSpec((1, d), lambda i, pos: (0, 0)),
            pl.BlockSpec((1, d), lambda i, pos: (0, 0)),
        ],
        out_specs=pl.BlockSpec((tm, d), lambda i, pos: (i, 0)),
        scratch_shapes=[pltpu.VMEM((2, 2 * tm, d), F32), pltpu.SemaphoreType.DMA((2,))],
    )
    return pl.pallas_call(
        functools.partial(_combine_kernel, tm=tm, alpha=alpha),
        grid_spec=grid_spec,
        out_shape=jax.ShapeDtypeStruct((t, d), F32),
        compiler_params=pltpu.CompilerParams(dimension_semantics=("arbitrary",), vmem_limit_bytes=VMEM_LIMIT),
        name="moe_combine",
    )(pos, y, x1, route_w, g, b)


def _tile_major(dest, tm):
    return dest.reshape(dest.shape[0] // tm, tm, 2).transpose(0, 2, 1).reshape(-1).astype(jnp.int32)


def _dispatch_plan(route_i, counts):
    t = route_i.shape[0]
    counts = counts[0, :N_EXPERTS].astype(jnp.int32)
    padded = ((counts + MOE_BLOCK - 1) // MOE_BLOCK) * MOE_BLOCK
    pad_end = jnp.cumsum(padded)
    pad_start = pad_end - padded
    dest = pad_start[route_i[:, 0:2]] + route_i[:, 2:4]
    n_blocks = -(-(2 * t) // MOE_BLOCK) + N_EXPERTS
    block_start = jnp.arange(n_blocks, dtype=jnp.int32) * MOE_BLOCK
    blk_e = jnp.minimum(jnp.sum((pad_end[None, :] <= block_start[:, None]).astype(jnp.int32), axis=1), N_EXPERTS - 1)
    pad_ends = jnp.concatenate([jnp.zeros((1,), jnp.int32), pad_end.astype(jnp.int32)])
    n_used = (pad_end[-1:] // MOE_BLOCK).astype(jnp.int32)
    return dest, pad_ends, blk_e, n_used, n_blocks * MOE_BLOCK


def _rope_tables(seq):
    half = HEAD_DIM // 2
    inv_freq = 1.0 / (ROPE_THETA ** (jnp.arange(half, dtype=F32) * 2.0 / HEAD_DIM))
    ang = jnp.arange(seq, dtype=F32)[:, None] * inv_freq[None, :]
    cos, sin = jnp.cos(ang), jnp.sin(ang)
    reps = LANES // HEAD_DIM
    return jnp.tile(jnp.concatenate([cos, cos], axis=1), (1, reps)), jnp.tile(jnp.concatenate([-sin, sin], axis=1), (1, reps))


def _pick_tile(n, want):
    while n % want:
        want //= 2
    return want


def kernel(x, w_in, b_in, conv_w, conv_b, conv_ln_g, conv_ln_b, lam_q1, lam_k1, lam_q2, lam_k2, subln_g, w_out, ln1_g, ln1_b, w_rg, b_rg, w_re, b_re, w_gate_e, w_up_e, w_down_e, ln2_g, ln2_b):
    batch, seq, d = x.shape
    depth = w_in.shape[0]
    t = batch * seq
    alpha = (2.0 * depth) ** 0.25
    cos_t, sin_t = _rope_tables(seq)
    tm = _pick_tile(seq, 512)
    ts = _pick_tile(seq, 256)
    tq = _pick_tile(seq, 512)
    tc = _pick_tile(seq, 256)
    td = _pick_tile(seq, 1024)
    row = lambda a: a.reshape(1, -1)

    x2 = x.reshape(t, d)
    for l in range(depth):
        lam_init = 0.8 - 0.6 * math.exp(-0.3 * l)
        u, q, k, v = _inproj(x2, w_in[l].astype(BF16), row(b_in[l]), cos_t, sin_t, seq=seq, tm=tm)
        u2 = _conv_module(u, conv_w[l].reshape(CONV_KERNEL, -1), row(conv_b[l]), row(conv_ln_g[l]),
                          row(conv_ln_b[l]), seq=seq, ts=ts)
        lam_params = jnp.stack([lam_q1[l], lam_k1[l], lam_q2[l], lam_k2[l]])
        o = _diff_attention(q, k, v, lam_params, row(subln_g[l]), batch=batch, seq=seq, tq=tq, lam_init=lam_init)
        w_router = jnp.concatenate([w_rg[l], w_re[l]], axis=1)
        w_router = jnp.pad(w_router, ((0, 0), (0, LANES - w_router.shape[1])))
        w_router_hi = w_router.astype(BF16)
        w_router = jnp.concatenate([w_router_hi, (w_router - w_router_hi.astype(F32)).astype(BF16)], axis=1)
        b_router = jnp.pad(jnp.concatenate([b_rg[l], b_re[l]]), (0, LANES - N_GROUPS - N_EXPERTS))
        x1, route_i, route_w, counts = _outproj_router(x2, u2, o, w_out[l].astype(BF16), row(ln1_g[l]), row(ln1_b[l]),
                                                       w_router, row(b_router), tm=tm, alpha=alpha)
        dest, pad_ends, blk_e, n_used, n_rows = _dispatch_plan(route_i, counts)
        xs = _dispatch(_tile_major(dest, td), pad_ends, x1, n_rows=n_rows, tm=td)
        y = _expert_mlp(blk_e, n_used, xs, w_gate_e, w_up_e, w_down_e, layer=l)
        x2 = _combine(_tile_major(dest, tc), y, x1, route_w, row(ln2_g[l]), row(ln2_b[l]), tm=tc, alpha=alpha)
    return x2.reshape(batch, seq, d)
```

```python
import functools
import math

import jax
import jax.numpy as jnp
from jax import lax
from jax.experimental import pallas as pl
from jax.experimental.pallas import tpu as pltpu

F32 = jnp.float32
BF16 = jnp.bfloat16

CHUNK = 64
CONV_KERNEL = 31
HEADS = 4
HEAD_DIM = 64
VALUE_DIM = 2 * HEAD_DIM
ROPE_THETA = 10000.0
N_GROUPS = 4
EXPERTS_PER_GROUP = 8
N_EXPERTS = N_GROUPS * EXPERTS_PER_GROUP
MOE_BLOCK = 256
LN_EPS = 1e-5
LANES = 128
SUBLANES = 8
CONV_HALO = 32
VMEM_LIMIT = 56 * 1024 * 1024


def _layernorm_rows(y, g, b):
    mu = jnp.mean(y, axis=-1, keepdims=True)
    d = y - mu
    var = jnp.mean(d * d, axis=-1, keepdims=True)
    return d * lax.rsqrt(var + LN_EPS) * g + b


def _inproj_kernel(x_ref, w_ref, b_ref, cos_ref, sin_ref, cw_ref, cb_ref, cg_ref, cbeta_ref,
                   u_ref, q_ref, k_ref, v_ref, buf_ref, sh_ref, *, cw, qk, tiles_per_seq, rb):
    i = pl.program_id(0)
    tm = x_ref.shape[0]
    xb = x_ref[...].astype(BF16)

    def proj(c0, n):
        return jnp.dot(xb, w_ref[:, c0:c0 + n], preferred_element_type=F32) + b_ref[:, c0:c0 + n]

    @pl.when(i % tiles_per_seq == 0)
    def _():
        buf_ref[0:CONV_HALO, :] = jnp.zeros((CONV_HALO, buf_ref.shape[1]), F32)

    @pl.when(i % tiles_per_seq != 0)
    def _():
        buf_ref[0:CONV_HALO, :] = buf_ref[tm:tm + CONV_HALO, :]

    ag = proj(0, 2 * cw)
    buf_ref[CONV_HALO:CONV_HALO + tm, :] = ag[:, :cw] * jax.nn.sigmoid(ag[:, cw:])

    first = CONV_HALO - (CONV_KERNEL - 1)
    span = tm + CONV_HALO - SUBLANES
    for p in range(1, SUBLANES):
        sh_ref[p - 1, 0:span, :] = buf_ref[p:p + span, :]

    bias = cb_ref[...]
    g = cg_ref[...]
    beta = cbeta_ref[...]
    for r0 in range(0, tm, rb):
        acc = jnp.broadcast_to(bias, (rb, bias.shape[1]))
        for j in range(CONV_KERNEL):
            p = (first + j) % SUBLANES
            a0 = r0 + first + j - p
            window = buf_ref[a0:a0 + rb, :] if p == 0 else sh_ref[p - 1, a0:a0 + rb, :]
            acc = acc + cw_ref[j:j + 1, :] * window
        y = _layernorm_rows(acc, g, beta)
        u_ref[r0:r0 + rb, :] = (y * jax.nn.sigmoid(y)).astype(u_ref.dtype)

    cos = cos_ref[...]
    sin = sin_ref[...]
    lane = lax.broadcasted_iota(jnp.int32, cos.shape, 1)
    first_half = (lane % HEAD_DIM) < (HEAD_DIM // 2)

    def rope(z, scale):
        outs = []
        for c in range(0, qk, LANES):
            zz = z[:, c:c + LANES]
            swapped = jnp.where(first_half, pltpu.roll(zz, LANES - HEAD_DIM // 2, 1),
                                pltpu.roll(zz, HEAD_DIM // 2, 1))
            outs.append((zz * cos + swapped * sin) * scale)
        return jnp.concatenate(outs, axis=1)

    q_ref[...] = rope(proj(2 * cw, qk), HEAD_DIM ** -0.5 * math.log2(math.e)).astype(BF16)
    k_ref[...] = rope(proj(2 * cw + qk, qk), 1.0).astype(BF16)
    v_ref[...] = proj(2 * cw + 2 * qk, v_ref.shape[1]).astype(BF16)


def _inproj(x2, w_bf, b, cos_t, sin_t, conv_w, conv_b, conv_g, conv_beta, *, seq, tm, rb=32):
    t, d = x2.shape
    n = w_bf.shape[1]
    cw = d // 2
    qk = 2 * HEADS * HEAD_DIM
    vw = HEADS * VALUE_DIM
    tiles_per_seq = seq // tm
    const = lambda i: (0, 0)
    return pl.pallas_call(
        functools.partial(_inproj_kernel, cw=cw, qk=qk, tiles_per_seq=tiles_per_seq, rb=rb),
        grid=(t // tm,),
        in_specs=[
            pl.BlockSpec((tm, d), lambda i: (i, 0)),
            pl.BlockSpec((d, n), const),
            pl.BlockSpec((1, n), const),
            pl.BlockSpec((tm, LANES), lambda i: (i % tiles_per_seq, 0)),
            pl.BlockSpec((tm, LANES), lambda i: (i % tiles_per_seq, 0)),
            pl.BlockSpec((CONV_KERNEL, cw), const),
            pl.BlockSpec((1, cw), const),
            pl.BlockSpec((1, cw), const),
            pl.BlockSpec((1, cw), const),
        ],
        out_specs=[
            pl.BlockSpec((tm, cw), lambda i: (i, 0)),
            pl.BlockSpec((tm, qk), lambda i: (i, 0)),
            pl.BlockSpec((tm, qk), lambda i: (i, 0)),
            pl.BlockSpec((tm, vw), lambda i: (i, 0)),
        ],
        out_shape=[
            jax.ShapeDtypeStruct((t, cw), BF16),
            jax.ShapeDtypeStruct((t, qk), BF16),
            jax.ShapeDtypeStruct((t, qk), BF16),
            jax.ShapeDtypeStruct((t, vw), BF16),
        ],
        scratch_shapes=[pltpu.VMEM((CONV_HALO + tm, cw), F32), pltpu.VMEM((SUBLANES - 1, CONV_HALO + tm, cw), F32)],
        compiler_params=pltpu.CompilerParams(dimension_semantics=("arbitrary",), vmem_limit_bytes=VMEM_LIMIT),
        name="inproj_conv",
    )(x2, w_bf, b, cos_t, sin_t, conv_w, conv_b, conv_g, conv_beta)


def _attn_kernel(lam_ref, g_ref, q_ref, k_ref, v_ref, o_ref, qq_scr, s_scr, p_scr, a_scr, m_scr, l_scr, acc_scr,
                 *, tq, rc, nh, lam_init):
    qi = pl.program_id(2)
    heads = range(nh)
    for h in heads:
        q = q_ref[:, h * VALUE_DIM:(h + 1) * VALUE_DIM]
        lane = lax.broadcasted_iota(jnp.int32, q.shape, 1)
        zero = jnp.zeros_like(q)
        qq_scr[h, 0:tq, :] = jnp.where(lane < HEAD_DIM, q, zero)
        qq_scr[h, tq:, :] = jnp.where(lane >= HEAD_DIM, q, zero)

    m_scr[...] = jnp.full(m_scr.shape, -jnp.inf, F32)
    l_scr[...] = jnp.zeros(l_scr.shape, F32)
    acc_scr[...] = jnp.zeros(acc_scr.shape, F32)

    def kv_rows(j):
        return pl.ds(pl.multiple_of(j * tq, tq), tq)

    def head_cols(h):
        return slice(h * VALUE_DIM, (h + 1) * VALUE_DIM)

    def scores(j, slot):
        for h in heads:
            s_scr[h, slot] = lax.dot_general(qq_scr[h], k_ref[kv_rows(j), head_cols(h)], (((1,), (1,)), ((), ())),
                                             preferred_element_type=F32)

    def softmax(slot, masked):
        for r0 in range(0, 2 * tq, rc):
            for h in heads:
                s = s_scr[h, slot, r0:r0 + rc, :]
                if masked:
                    row = (lax.broadcasted_iota(jnp.int32, s.shape, 0) + r0) % tq
                    col = lax.broadcasted_iota(jnp.int32, s.shape, 1)
                    s = jnp.where((col // CHUNK) <= (row // CHUNK), s, -jnp.inf)
                m_prev = m_scr[h, r0:r0 + rc, :]
                m_new = jnp.maximum(m_prev, jnp.max(s, axis=-1, keepdims=True))
                alpha = jnp.exp2(m_prev - m_new)
                p = jnp.exp2(s - jnp.tile(m_new, (1, tq // LANES)))
                l_scr[h, r0:r0 + rc, :] = alpha * l_scr[h, r0:r0 + rc, :] + jnp.sum(p, axis=-1, keepdims=True)
                m_scr[h, r0:r0 + rc, :] = m_new
                a_scr[h, slot, r0:r0 + rc, :] = alpha
                p_scr[h, slot, r0:r0 + rc, :] = p.astype(BF16)

    def values(j, slot):
        for h in heads:
            acc_scr[h] = a_scr[h, slot] * acc_scr[h] + jnp.dot(p_scr[h, slot], v_ref[kv_rows(j), head_cols(h)],
                                                               preferred_element_type=F32)

    def step(i, par):
        scores(i + 1, 1 - par)
        softmax(par, False)
        values(i - 1, 1 - par)

    def finish(par):
        softmax(par, True)
        values(qi - 1, 1 - par)
        values(qi, par)

    @pl.when(qi == 0)
    def _():
        scores(0, 0)
        softmax(0, True)
        values(0, 0)

    @pl.when(qi > 0)
    def _():
        scores(0, 0)
        scores(1, 1)
        softmax(0, False)

    def body(i, carry):
        for par in (0, 1):
            @pl.when(i % 2 == par)
            def _():
                step(i, par)
        return carry

    lax.fori_loop(1, qi, body, 0)

    for par in (0, 1):
        @pl.when((qi > 0) & (qi % 2 == par))
        def _():
            finish(par)

    lp = lam_ref[...]
    lam = (jnp.exp(jnp.sum(lp[0:1] * lp[1:2], axis=-1, keepdims=True))
           - jnp.exp(jnp.sum(lp[2:3] * lp[3:4], axis=-1, keepdims=True)) + lam_init)
    for h in heads:
        o = acc_scr[h] / l_scr[h]
        o = o[:tq] - lam * o[tq:]
        ms = jnp.mean(o * o, axis=-1, keepdims=True)
        o_ref[:, head_cols(h)] = (o * lax.rsqrt(ms + LN_EPS) * g_ref[...] * (1.0 - lam_init)).astype(o_ref.dtype)


def _diff_attention(q, k, v, lam_params, subln_g, *, batch, seq, tq, lam_init, nh=1):
    t = q.shape[0]
    nq = seq // tq
    w = nh * VALUE_DIM
    return pl.pallas_call(
        functools.partial(_attn_kernel, tq=tq, rc=min(256, tq), nh=nh, lam_init=lam_init),
        grid=(batch, HEADS // nh, nq),
        in_specs=[
            pl.BlockSpec((4, HEAD_DIM), lambda b, h, i: (0, 0)),
            pl.BlockSpec((1, VALUE_DIM), lambda b, h, i: (0, 0)),
            pl.BlockSpec((tq, w), lambda b, h, i: (b * nq + i, h)),
            pl.BlockSpec((seq, w), lambda b, h, i: (b, h)),
            pl.BlockSpec((seq, w), lambda b, h, i: (b, h)),
        ],
        out_specs=pl.BlockSpec((tq, w), lambda b, h, i: (b * nq + i, h)),
        out_shape=jax.ShapeDtypeStruct((t, HEADS * VALUE_DIM), BF16),
        scratch_shapes=[
            pltpu.VMEM((nh, 2 * tq, VALUE_DIM), BF16),
            pltpu.VMEM((nh, 2, 2 * tq, tq), F32),
            pltpu.VMEM((nh, 2, 2 * tq, tq), BF16),
            pltpu.VMEM((nh, 2, 2 * tq, LANES), F32),
            pltpu.VMEM((nh, 2 * tq, LANES), F32),
            pltpu.VMEM((nh, 2 * tq, LANES), F32),
            pltpu.VMEM((nh, 2 * tq, VALUE_DIM), F32),
        ],
        compiler_params=pltpu.CompilerParams(dimension_semantics=("arbitrary", "arbitrary", "arbitrary"),
                                             vmem_limit_bytes=VMEM_LIMIT),
        name="diff_attention",
    )(lam_params, subln_g, q, k, v)


def _outproj_kernel(x_ref, u_ref, o_ref, w_ref, g_ref, b_ref, wr_ref, br_ref,
                    x1_ref, ri_ref, rw_ref, cnt_ref, carry_scr, *, alpha, cw):
    acc = jnp.dot(u_ref[...], w_ref[0:cw, :], preferred_element_type=F32)
    acc = acc + jnp.dot(o_ref[...], w_ref[cw:, :], preferred_element_type=F32)
    x1 = _layernorm_rows(alpha * x_ref[...] + acc, g_ref[...], b_ref[...])
    x1_ref[...] = x1

    x_hi = x1.astype(BF16)
    x_lo = (x1 - x_hi.astype(F32)).astype(BF16)
    hi = jnp.dot(x_hi, wr_ref[...], preferred_element_type=F32)
    logits = (hi[:, :LANES] + hi[:, LANES:] + jnp.dot(x_lo, wr_ref[:, :LANES], preferred_element_type=F32)
              + br_ref[...])
    lane = lax.broadcasted_iota(jnp.int32, logits.shape, 1)
    neg = -jnp.inf
    gmask = lane < N_GROUPS
    lg = jnp.where(gmask, logits, neg)
    gmax = jnp.max(lg, axis=-1, keepdims=True)
    grp = jnp.min(jnp.where(lg == gmax, lane, LANES), axis=-1, keepdims=True)
    gsum = jnp.sum(jnp.where(gmask, jnp.exp(lg - gmax), 0.0), axis=-1, keepdims=True)
    gate = 1.0 / gsum
    emask = (lane >= N_GROUPS) & (lane < N_GROUPS + N_EXPERTS) & (((lane - N_GROUPS) // EXPERTS_PER_GROUP) == grp)
    ev = jnp.where(emask, logits, neg)
    v0 = jnp.max(ev, axis=-1, keepdims=True)
    i0 = jnp.min(jnp.where(emask & (ev == v0), lane, LANES), axis=-1, keepdims=True)
    emask1 = emask & (lane != i0)
    ev1 = jnp.where(emask1, logits, neg)
    v1 = jnp.max(ev1, axis=-1, keepdims=True)
    i1 = jnp.min(jnp.where(emask1 & (ev1 == v1), lane, LANES), axis=-1, keepdims=True)
    e1 = jnp.exp(v1 - v0)
    w0 = gate * (1.0 / (1.0 + e1))
    w1 = gate * (e1 / (1.0 + e1))

    @pl.when(pl.program_id(0) == 0)
    def _():
        carry_scr[...] = jnp.zeros(carry_scr.shape, F32)

    tm = logits.shape[0]
    oh0 = (lane == i0 - N_GROUPS).astype(F32)
    oh1 = (lane == i1 - N_GROUPS).astype(F32)
    both = oh0 + oh1
    tri = (lax.broadcasted_iota(jnp.int32, (tm, tm), 0) > lax.broadcasted_iota(jnp.int32, (tm, tm), 1)).astype(BF16)
    before = jnp.dot(tri, both.astype(BF16), preferred_element_type=F32) + carry_scr[...]
    rank0 = jnp.sum(oh0 * before, axis=-1, keepdims=True).astype(jnp.int32)
    rank1 = jnp.sum(oh1 * before, axis=-1, keepdims=True).astype(jnp.int32)
    carry_scr[...] = carry_scr[...] + jnp.sum(both, axis=0, keepdims=True)
    cnt_ref[...] = carry_scr[...]

    ri_ref[...] = jnp.where(lane == 0, i0 - N_GROUPS, jnp.where(lane == 1, i1 - N_GROUPS,
                            jnp.where(lane == 2, rank0, jnp.where(lane == 3, rank1, 0))))
    rw_ref[...] = jnp.where(lane == 0, w0, jnp.where(lane == 1, w1, 0.0))


def _outproj_router(x2, u2, o, w_bf, g, b, wr, br, *, tm, alpha):
    t, d = x2.shape
    cw = u2.shape[1]
    row = lambda i: (i, 0)
    const = lambda i: (0, 0)
    return pl.pallas_call(
        functools.partial(_outproj_kernel, alpha=alpha, cw=cw),
        grid=(t // tm,),
        in_specs=[
            pl.BlockSpec((tm, d), row),
            pl.BlockSpec((tm, cw), row),
            pl.BlockSpec((tm, o.shape[1]), row),
            pl.BlockSpec(w_bf.shape, const),
            pl.BlockSpec((1, d), const),
            pl.BlockSpec((1, d), const),
            pl.BlockSpec((d, 2 * LANES), const),
            pl.BlockSpec((1, LANES), const),
        ],
        out_specs=[
            pl.BlockSpec((tm, d), row),
            pl.BlockSpec((tm, LANES), row),
            pl.BlockSpec((tm, LANES), row),
            pl.BlockSpec((1, LANES), const),
        ],
        out_shape=[
            jax.ShapeDtypeStruct((t, d), F32),
            jax.ShapeDtypeStruct((t, LANES), jnp.int32),
            jax.ShapeDtypeStruct((t, LANES), F32),
            jax.ShapeDtypeStruct((1, LANES), F32),
        ],
        scratch_shapes=[pltpu.VMEM((1, LANES), F32)],
        compiler_params=pltpu.CompilerParams(dimension_semantics=("arbitrary",), vmem_limit_bytes=VMEM_LIMIT),
        name="outproj_router",
    )(x2, u2, o, w_bf, g, b, wr, br)


def _dispatch_kernel(pos_ref, pend_ref, x_ref, xs_hbm, zbuf, sem, zsem, *, tm):
    i = pl.program_id(0)

    def zero_copy(e):
        start = pl.multiple_of(pend_ref[e + 1] - MOE_BLOCK, MOE_BLOCK)
        return pltpu.make_async_copy(zbuf, xs_hbm.at[pl.ds(start, MOE_BLOCK)], zsem)

    def zero_block(b):
        return pltpu.make_async_copy(zbuf, xs_hbm.at[pl.ds(pl.multiple_of(b * MOE_BLOCK, MOE_BLOCK), MOE_BLOCK)], zsem)

    @pl.when(i == 0)
    def _():
        zbuf[...] = jnp.zeros(zbuf.shape, zbuf.dtype)
        first_unused = pend_ref[N_EXPERTS] // MOE_BLOCK
        n_blocks = xs_hbm.shape[0] // MOE_BLOCK
        for e in range(N_EXPERTS):
            @pl.when(pend_ref[e + 1] > pend_ref[e])
            def _():
                zero_copy(e).start()

        def start_unused(b, c):
            zero_block(b).start()
            return c

        def wait_unused(b, c):
            zero_block(b).wait()
            return c

        lax.fori_loop(first_unused, n_blocks, start_unused, 0)
        for e in range(N_EXPERTS):
            @pl.when(pend_ref[e + 1] > pend_ref[e])
            def _():
                zero_copy(e).wait()
        lax.fori_loop(first_unused, n_blocks, wait_unused, 0)

    def row_copy(r, dst_row):
        return pltpu.make_async_copy(x_ref.at[pl.ds(r % tm, 1)], xs_hbm.at[pl.ds(dst_row, 1)], sem)

    for r in range(2 * tm):
        row_copy(r, pos_ref[i * 2 * tm + r]).start(priority=r % 2)
    for r in range(2 * tm):
        row_copy(r, 0).wait()


def _dispatch(pos, pad_ends, x1, *, n_rows, tm):
    t, d = x1.shape
    grid_spec = pltpu.PrefetchScalarGridSpec(
        num_scalar_prefetch=2,
        grid=(t // tm,),
        in_specs=[pl.BlockSpec((tm, d), lambda i, pos, pe: (i, 0))],
        out_specs=pl.BlockSpec(memory_space=pl.ANY),
        scratch_shapes=[pltpu.VMEM((MOE_BLOCK, d), x1.dtype), pltpu.SemaphoreType.DMA(()), pltpu.SemaphoreType.DMA(())],
    )
    return pl.pallas_call(
        functools.partial(_dispatch_kernel, tm=tm),
        grid_spec=grid_spec,
        out_shape=jax.ShapeDtypeStruct((n_rows, d), x1.dtype),
        compiler_params=pltpu.CompilerParams(dimension_semantics=("arbitrary",)),
        name="moe_dispatch",
    )(pos, pad_ends, x1)


def _expert_kernel(blk_e_ref, nused_ref, slot_ref, next_ref, xs_ref, wg_hbm, wu_hbm, wd_hbm, y_ref,
                   wg_f32, wu_f32, wd_f32, wg_bf, wu_bf, wd_bf, sem, *, layer):
    i = pl.program_id(0)

    def fetch(e, s):
        return [pltpu.make_async_copy(src.at[layer, e], dst.at[s], sem.at[s])
                for src, dst in ((wg_hbm, wg_f32), (wu_hbm, wu_f32), (wd_hbm, wd_f32))]

    @pl.when(i < nused_ref[0])
    def _():
        @pl.when((i == 0) | (blk_e_ref[i] != blk_e_ref[jnp.maximum(i - 1, 0)]))
        def _():
            e = blk_e_ref[i]
            s = slot_ref[i]
            nxt = next_ref[i]

            @pl.when(i == 0)
            def _():
                for c in fetch(e, s):
                    c.start()

            for c in fetch(e, s):
                c.wait()

            @pl.when(nxt >= 0)
            def _():
                for c in fetch(nxt, 1 - s):
                    c.start()

            wg_bf[...] = wg_f32[s].astype(BF16)
            wu_bf[...] = wu_f32[s].astype(BF16)
            wd_bf[...] = wd_f32[s].astype(BF16)

        xb = xs_ref[...].astype(BF16)
        gate = jnp.dot(xb, wg_bf[...], preferred_element_type=F32)
        up = jnp.dot(xb, wu_bf[...], preferred_element_type=F32)
        h = (gate * jax.nn.sigmoid(gate)) * up
        y_ref[...] = jnp.dot(h.astype(BF16), wd_bf[...], preferred_element_type=F32)

    @pl.when(i >= nused_ref[0])
    def _():
        y_ref[...] = jnp.zeros(y_ref.shape, y_ref.dtype)


def _expert_mlp(blk_e, n_used, run_slot, next_e, xs, wg, wu, wd, *, layer):
    n_rows, d = xs.shape
    n_blocks = n_rows // MOE_BLOCK
    de = wg.shape[3]
    live = lambda i, be, nu, rs, ne: jnp.maximum(jnp.minimum(i, nu[0] - 1), 0)
    hbm = pl.BlockSpec(memory_space=pl.ANY)
    grid_spec = pltpu.PrefetchScalarGridSpec(
        num_scalar_prefetch=4,
        grid=(n_blocks,),
        in_specs=[pl.BlockSpec((MOE_BLOCK, d), lambda i, be, nu, rs, ne: (live(i, be, nu, rs, ne), 0)), hbm, hbm, hbm],
        out_specs=pl.BlockSpec((MOE_BLOCK, d), lambda i, be, nu, rs, ne: (i, 0)),
        scratch_shapes=[
            pltpu.VMEM((2, d, de), F32), pltpu.VMEM((2, d, de), F32), pltpu.VMEM((2, de, d), F32),
            pltpu.VMEM((d, de), BF16), pltpu.VMEM((d, de), BF16), pltpu.VMEM((de, d), BF16),
            pltpu.SemaphoreType.DMA((2,)),
        ],
    )
    return pl.pallas_call(
        functools.partial(_expert_kernel, layer=layer),
        grid_spec=grid_spec,
        out_shape=jax.ShapeDtypeStruct((n_rows, d), F32),
        compiler_params=pltpu.CompilerParams(dimension_semantics=("arbitrary",), vmem_limit_bytes=VMEM_LIMIT),
        name="expert_mlp",
    )(blk_e, n_used, run_slot, next_e, xs, wg, wu, wd)


def _row_gather_copy(src_hbm, dst_buf, sem, slot, row, src_row):
    return pltpu.make_async_copy(src_hbm.at[pl.ds(src_row, 1)], dst_buf.at[slot, pl.ds(row, 1)], sem.at[slot])


def _combine_kernel(pos_ref, y_hbm, x_ref, rw_ref, g_ref, b_ref, o_ref, ybuf, sem, *, tm, alpha):
    i = pl.program_id(0)
    n = pl.num_programs(0)
    slot = i % 2

    def issue(tile, s):
        for r in range(2 * tm):
            _row_gather_copy(y_hbm, ybuf, sem, s, r, pos_ref[tile * 2 * tm + r]).start(priority=r % 2)

    def drain(s):
        for r in range(2 * tm):
            _row_gather_copy(y_hbm, ybuf, sem, s, r, 0).wait()

    @pl.when(i == 0)
    def _():
        issue(0, 0)
        issue(jnp.minimum(1, n - 1), 1)

    drain(slot)
    yy = ybuf[slot]
    rw = rw_ref[...]
    f = yy[:tm] * rw[:, 0:1] + yy[tm:] * rw[:, 1:2]
    o_ref[...] = _layernorm_rows(alpha * x_ref[...] + f, g_ref[...], b_ref[...])
    issue(jnp.minimum(i + 2, n - 1), slot)

    @pl.when(i == n - 1)
    def _():
        drain(slot)
        drain(1 - slot)


def _combine(pos, y, x1, route_w, g, b, *, tm, alpha):
    t, d = x1.shape
    grid_spec = pltpu.PrefetchScalarGridSpec(
        num_scalar_prefetch=1,
        grid=(t // tm,),
        in_specs=[
            pl.BlockSpec(memory_space=pl.ANY),
            pl.BlockSpec((tm, d), lambda i, pos: (i, 0)),
            pl.BlockSpec((tm, LANES), lambda i, pos: (i, 0)),
            pl.BlockSpec((1, d), lambda i, pos: (0, 0)),
            pl.BlockSpec((1, d), lambda i, pos: (0, 0)),
        ],
        out_specs=pl.BlockSpec((tm, d), lambda i, pos: (i, 0)),
        scratch_shapes=[pltpu.VMEM((2, 2 * tm, d), F32), pltpu.SemaphoreType.DMA((2,))],
    )
    return pl.pallas_call(
        functools.partial(_combine_kernel, tm=tm, alpha=alpha),
        grid_spec=grid_spec,
        out_shape=jax.ShapeDtypeStruct((t, d), F32),
        compiler_params=pltpu.CompilerParams(dimension_semantics=("arbitrary",), vmem_limit_bytes=VMEM_LIMIT),
        name="moe_combine",
    )(pos, y, x1, route_w, g, b)


def _tile_major(dest, tm):
    return dest.reshape(dest.shape[0] // tm, tm, 2).transpose(0, 2, 1).reshape(-1).astype(jnp.int32)


def _dispatch_plan(route_i, counts):
    t = route_i.shape[0]
    counts = counts[0, :N_EXPERTS].astype(jnp.int32)
    padded = ((counts + MOE_BLOCK - 1) // MOE_BLOCK) * MOE_BLOCK
    pad_end = jnp.cumsum(padded)
    pad_start = pad_end - padded
    dest = pad_start[route_i[:, 0:2]] + route_i[:, 2:4]
    n_blocks = -(-(2 * t) // MOE_BLOCK) + N_EXPERTS
    block_start = jnp.arange(n_blocks, dtype=jnp.int32) * MOE_BLOCK
    blk_e = jnp.minimum(jnp.sum((pad_end[None, :] <= block_start[:, None]).astype(jnp.int32), axis=1), N_EXPERTS - 1)
    pad_ends = jnp.concatenate([jnp.zeros((1,), jnp.int32), pad_end.astype(jnp.int32)])
    n_used = (pad_end[-1:] // MOE_BLOCK).astype(jnp.int32)
    experts = jnp.arange(N_EXPERTS, dtype=jnp.int32)
    nonempty = counts > 0
    run_idx = jnp.cumsum(nonempty.astype(jnp.int32)) - 1
    later = jnp.where(nonempty[None, :] & (experts[None, :] > experts[:, None]), experts[None, :], N_EXPERTS)
    next_nonempty = jnp.min(later, axis=1)
    next_nonempty = jnp.where(next_nonempty < N_EXPERTS, next_nonempty, -1).astype(jnp.int32)
    run_slot = (run_idx % 2)[blk_e].astype(jnp.int32)
    next_e = next_nonempty[blk_e]
    return dest, pad_ends, blk_e, n_used, run_slot, next_e, n_blocks * MOE_BLOCK


def _rope_tables(seq):
    half = HEAD_DIM // 2
    inv_freq = 1.0 / (ROPE_THETA ** (jnp.arange(half, dtype=F32) * 2.0 / HEAD_DIM))
    ang = jnp.arange(seq, dtype=F32)[:, None] * inv_freq[None, :]
    cos, sin = jnp.cos(ang), jnp.sin(ang)
    reps = LANES // HEAD_DIM
    return jnp.tile(jnp.concatenate([cos, cos], axis=1), (1, reps)), jnp.tile(jnp.concatenate([-sin, sin], axis=1), (1, reps))


def _pick_tile(n, want):
    while n % want:
        want //= 2
    return want


def kernel(x, w_in, b_in, conv_w, conv_b, conv_ln_g, conv_ln_b, lam_q1, lam_k1, lam_q2, lam_k2, subln_g, w_out, ln1_g, ln1_b, w_rg, b_rg, w_re, b_re, w_gate_e, w_up_e, w_down_e, ln2_g, ln2_b):
    batch, seq, d = x.shape
    depth = w_in.shape[0]
    t = batch * seq
    alpha = (2.0 * depth) ** 0.25
    cos_t, sin_t = _rope_tables(seq)
    tm = _pick_tile(seq, 512)
    tq = _pick_tile(seq, 512)
    tc = _pick_tile(seq, 256)
    td = _pick_tile(seq, 1024)
    row = lambda a: a.reshape(1, -1)

    x2 = x.reshape(t, d)
    for l in range(depth):
        lam_init = 0.8 - 0.6 * math.exp(-0.3 * l)
        u2, q, k, v = _inproj(x2, w_in[l].astype(BF16), row(b_in[l]), cos_t, sin_t,
                              conv_w[l].reshape(CONV_KERNEL, -1), row(conv_b[l]), row(conv_ln_g[l]),
                              row(conv_ln_b[l]), seq=seq, tm=tm)
        lam_params = jnp.stack([lam_q1[l], lam_k1[l], lam_q2[l], lam_k2[l]])
        o = _diff_attention(q, k, v, lam_params, row(subln_g[l]), batch=batch, seq=seq, tq=tq, lam_init=lam_init)
        w_router = jnp.concatenate([w_rg[l], w_re[l]], axis=1)
        w_router = jnp.pad(w_router, ((0, 0), (0, LANES - w_router.shape[1])))
        w_router_hi = w_router.astype(BF16)
        w_router = jnp.concatenate([w_router_hi, (w_router - w_router_hi.astype(F32)).astype(BF16)], axis=1)
        b_router = jnp.pad(jnp.concatenate([b_rg[l], b_re[l]]), (0, LANES - N_GROUPS - N_EXPERTS))
        x1, route_i, route_w, counts = _outproj_router(x2, u2, o, w_out[l].astype(BF16), row(ln1_g[l]), row(ln1_b[l]),
                                                       w_router, row(b_router), tm=tm, alpha=alpha)
        dest, pad_ends, blk_e, n_used, run_slot, next_e, n_rows = _dispatch_plan(route_i, counts)
        xs = _dispatch(_tile_major(dest, td), pad_ends, x1, n_rows=n_rows, tm=td)
        y = _expert_mlp(blk_e, n_used, run_slot, next_e, xs, w_gate_e, w_up_e, w_down_e, layer=l)
        x2 = _combine(_tile_major(dest, tc), y, x1, route_w, row(ln2_g[l]), row(ln2_b[l]), tm=tc, alpha=alpha)
    return x2.reshape(batch, seq, d)
```

```python
import functools
import math

import jax
import jax.numpy as jnp
from jax import lax
from jax.experimental import pallas as pl
from jax.experimental.pallas import tpu as pltpu

F32 = jnp.float32
BF16 = jnp.bfloat16

CHUNK = 64
CONV_KERNEL = 31
HEADS = 4
HEAD_DIM = 64
VALUE_DIM = 2 * HEAD_DIM
ROPE_THETA = 10000.0
N_GROUPS = 4
EXPERTS_PER_GROUP = 8
N_EXPERTS = N_GROUPS * EXPERTS_PER_GROUP
MOE_BLOCK = 256
LN_EPS = 1e-5
LANES = 128
SUBLANES = 8
CONV_HALO = 32
VMEM_LIMIT = 56 * 1024 * 1024


def _layernorm_rows(y, g, b):
    mu = jnp.mean(y, axis=-1, keepdims=True)
    d = y - mu
    var = jnp.mean(d * d, axis=-1, keepdims=True)
    return d * lax.rsqrt(var + LN_EPS) * g + b


def _inproj_kernel(x_ref, w_ref, b_ref, cos_ref, sin_ref, cw_ref, cb_ref, cg_ref, cbeta_ref,
                   u_ref, q_ref, k_ref, v_ref, buf_ref, sh_ref, *, cw, qk, tiles_per_seq, rb):
    i = pl.program_id(0)
    tm = x_ref.shape[0]
    xb = x_ref[...].astype(BF16)

    def proj(c0, n):
        return jnp.dot(xb, w_ref[:, c0:c0 + n], preferred_element_type=F32) + b_ref[:, c0:c0 + n]

    @pl.when(i % tiles_per_seq == 0)
    def _():
        buf_ref[0:CONV_HALO, :] = jnp.zeros((CONV_HALO, buf_ref.shape[1]), F32)

    @pl.when(i % tiles_per_seq != 0)
    def _():
        buf_ref[0:CONV_HALO, :] = buf_ref[tm:tm + CONV_HALO, :]

    ag = proj(0, 2 * cw)
    buf_ref[CONV_HALO:CONV_HALO + tm, :] = ag[:, :cw] * jax.nn.sigmoid(ag[:, cw:])

    first = CONV_HALO - (CONV_KERNEL - 1)
    span = tm + CONV_HALO - SUBLANES
    for p in range(1, SUBLANES):
        sh_ref[p - 1, 0:span, :] = buf_ref[p:p + span, :]

    bias = cb_ref[...]
    g = cg_ref[...]
    beta = cbeta_ref[...]
    for r0 in range(0, tm, rb):
        acc = jnp.broadcast_to(bias, (rb, bias.shape[1]))
        for j in range(CONV_KERNEL):
            p = (first + j) % SUBLANES
            a0 = r0 + first + j - p
            window = buf_ref[a0:a0 + rb, :] if p == 0 else sh_ref[p - 1, a0:a0 + rb, :]
            acc = acc + cw_ref[j:j + 1, :] * window
        y = _layernorm_rows(acc, g, beta)
        u_ref[r0:r0 + rb, :] = (y * jax.nn.sigmoid(y)).astype(u_ref.dtype)

    cos = cos_ref[...]
    sin = sin_ref[...]
    lane = lax.broadcasted_iota(jnp.int32, cos.shape, 1)
    first_half = (lane % HEAD_DIM) < (HEAD_DIM // 2)

    def rope(z, scale):
        outs = []
        for c in range(0, qk, LANES):
            zz = z[:, c:c + LANES]
            swapped = jnp.where(first_half, pltpu.roll(zz, LANES - HEAD_DIM // 2, 1),
                                pltpu.roll(zz, HEAD_DIM // 2, 1))
            outs.append((zz * cos + swapped * sin) * scale)
        return jnp.concatenate(outs, axis=1)

    q_ref[...] = rope(proj(2 * cw, qk), HEAD_DIM ** -0.5 * math.log2(math.e)).astype(BF16)
    k_ref[...] = rope(proj(2 * cw + qk, qk), 1.0).astype(BF16)
    v_ref[...] = proj(2 * cw + 2 * qk, v_ref.shape[1]).astype(BF16)


def _inproj(x2, w_bf, b, cos_t, sin_t, conv_w, conv_b, conv_g, conv_beta, *, seq, tm, rb=32):
    t, d = x2.shape
    n = w_bf.shape[1]
    cw = d // 2
    qk = 2 * HEADS * HEAD_DIM
    vw = HEADS * VALUE_DIM
    tiles_per_seq = seq // tm
    const = lambda i: (0, 0)
    return pl.pallas_call(
        functools.partial(_inproj_kernel, cw=cw, qk=qk, tiles_per_seq=tiles_per_seq, rb=rb),
        grid=(t // tm,),
        in_specs=[
            pl.BlockSpec((tm, d), lambda i: (i, 0)),
            pl.BlockSpec((d, n), const),
            pl.BlockSpec((1, n), const),
            pl.BlockSpec((tm, LANES), lambda i: (i % tiles_per_seq, 0)),
            pl.BlockSpec((tm, LANES), lambda i: (i % tiles_per_seq, 0)),
            pl.BlockSpec((CONV_KERNEL, cw), const),
            pl.BlockSpec((1, cw), const),
            pl.BlockSpec((1, cw), const),
            pl.BlockSpec((1, cw), const),
        ],
        out_specs=[
            pl.BlockSpec((tm, cw), lambda i: (i, 0)),
            pl.BlockSpec((tm, qk), lambda i: (i, 0)),
            pl.BlockSpec((tm, qk), lambda i: (i, 0)),
            pl.BlockSpec((tm, vw), lambda i: (i, 0)),
        ],
        out_shape=[
            jax.ShapeDtypeStruct((t, cw), BF16),
            jax.ShapeDtypeStruct((t, qk), BF16),
            jax.ShapeDtypeStruct((t, qk), BF16),
            jax.ShapeDtypeStruct((t, vw), BF16),
        ],
        scratch_shapes=[pltpu.VMEM((CONV_HALO + tm, cw), F32), pltpu.VMEM((SUBLANES - 1, CONV_HALO + tm, cw), F32)],
        compiler_params=pltpu.CompilerParams(dimension_semantics=("arbitrary",), vmem_limit_bytes=VMEM_LIMIT),
        name="inproj_conv",
    )(x2, w_bf, b, cos_t, sin_t, conv_w, conv_b, conv_g, conv_beta)


def _attn_kernel(lam_ref, g_ref, q_ref, k_ref, v_ref, o_ref, qq_scr, s_scr, p_scr, a_scr, m_scr, l_scr, acc_scr,
                 *, tq, rc, nh, lam_init):
    qi = pl.program_id(2)
    heads = range(nh)
    for h in heads:
        q = q_ref[:, h * VALUE_DIM:(h + 1) * VALUE_DIM]
        lane = lax.broadcasted_iota(jnp.int32, q.shape, 1)
        zero = jnp.zeros_like(q)
        qq_scr[h, 0:tq, :] = jnp.where(lane < HEAD_DIM, q, zero)
        qq_scr[h, tq:, :] = jnp.where(lane >= HEAD_DIM, q, zero)

    m_scr[...] = jnp.full(m_scr.shape, -jnp.inf, F32)
    l_scr[...] = jnp.zeros(l_scr.shape, F32)
    acc_scr[...] = jnp.zeros(acc_scr.shape, F32)

    def kv_rows(j):
        return pl.ds(pl.multiple_of(j * tq, tq), tq)

    def head_cols(h):
        return slice(h * VALUE_DIM, (h + 1) * VALUE_DIM)

    def scores(j, slot):
        for h in heads:
            s_scr[h, slot] = lax.dot_general(qq_scr[h], k_ref[kv_rows(j), head_cols(h)], (((1,), (1,)), ((), ())),
                                             preferred_element_type=F32)

    def softmax(slot, masked):
        for r0 in range(0, 2 * tq, rc):
            for h in heads:
                s = s_scr[h, slot, r0:r0 + rc, :]
                if masked:
                    row = (lax.broadcasted_iota(jnp.int32, s.shape, 0) + r0) % tq
                    col = lax.broadcasted_iota(jnp.int32, s.shape, 1)
                    s = jnp.where((col // CHUNK) <= (row // CHUNK), s, -jnp.inf)
                m_prev = m_scr[h, r0:r0 + rc, :]
                m_new = jnp.maximum(m_prev, jnp.max(s, axis=-1, keepdims=True))
                alpha = jnp.exp2(m_prev - m_new)
                p = jnp.exp2(s - jnp.tile(m_new, (1, tq // LANES)))
                l_scr[h, r0:r0 + rc, :] = alpha * l_scr[h, r0:r0 + rc, :] + jnp.sum(p, axis=-1, keepdims=True)
                m_scr[h, r0:r0 + rc, :] = m_new
                a_scr[h, slot, r0:r0 + rc, :] = alpha
                p_scr[h, slot, r0:r0 + rc, :] = p.astype(BF16)

    def values(j, slot):
        for h in heads:
            acc_scr[h] = a_scr[h, slot] * acc_scr[h] + jnp.dot(p_scr[h, slot], v_ref[kv_rows(j), head_cols(h)],
                                                               preferred_element_type=F32)

    def step(i, par):
        scores(i + 1, 1 - par)
        softmax(par, False)
        values(i - 1, 1 - par)

    def finish(par):
        softmax(par, True)
        values(qi - 1, 1 - par)
        values(qi, par)

    @pl.when(qi == 0)
    def _():
        scores(0, 0)
        softmax(0, True)
        values(0, 0)

    @pl.when(qi > 0)
    def _():
        scores(0, 0)
        scores(1, 1)
        softmax(0, False)

    def body(i, carry):
        for par in (0, 1):
            @pl.when(i % 2 == par)
            def _():
                step(i, par)
        return carry

    lax.fori_loop(1, qi, body, 0)

    for par in (0, 1):
        @pl.when((qi > 0) & (qi % 2 == par))
        def _():
            finish(par)

    lp = lam_ref[...]
    lam = (jnp.exp(jnp.sum(lp[0:1] * lp[1:2], axis=-1, keepdims=True))
           - jnp.exp(jnp.sum(lp[2:3] * lp[3:4], axis=-1, keepdims=True)) + lam_init)
    for h in heads:
        o = acc_scr[h] / l_scr[h]
        o = o[:tq] - lam * o[tq:]
        ms = jnp.mean(o * o, axis=-1, keepdims=True)
        o_ref[:, head_cols(h)] = (o * lax.rsqrt(ms + LN_EPS) * g_ref[...] * (1.0 - lam_init)).astype(o_ref.dtype)


def _diff_attention(q, k, v, lam_params, subln_g, *, batch, seq, tq, lam_init, nh=1):
    t = q.shape[0]
    nq = seq // tq
    w = nh * VALUE_DIM
    return pl.pallas_call(
        functools.partial(_attn_kernel, tq=tq, rc=min(256, tq), nh=nh, lam_init=lam_init),
        grid=(batch, HEADS // nh, nq),
        in_specs=[
            pl.BlockSpec((4, HEAD_DIM), lambda b, h, i: (0, 0)),
            pl.BlockSpec((1, VALUE_DIM), lambda b, h, i: (0, 0)),
            pl.BlockSpec((tq, w), lambda b, h, i: (b * nq + i, h)),
            pl.BlockSpec((seq, w), lambda b, h, i: (b, h)),
            pl.BlockSpec((seq, w), lambda b, h, i: (b, h)),
        ],
        out_specs=pl.BlockSpec((tq, w), lambda b, h, i: (b * nq + i, h)),
        out_shape=jax.ShapeDtypeStruct((t, HEADS * VALUE_DIM), BF16),
        scratch_shapes=[
            pltpu.VMEM((nh, 2 * tq, VALUE_DIM), BF16),
            pltpu.VMEM((nh, 2, 2 * tq, tq), F32),
            pltpu.VMEM((nh, 2, 2 * tq, tq), BF16),
            pltpu.VMEM((nh, 2, 2 * tq, LANES), F32),
            pltpu.VMEM((nh, 2 * tq, LANES), F32),
            pltpu.VMEM((nh, 2 * tq, LANES), F32),
            pltpu.VMEM((nh, 2 * tq, VALUE_DIM), F32),
        ],
        compiler_params=pltpu.CompilerParams(dimension_semantics=("arbitrary", "arbitrary", "arbitrary"),
                                             vmem_limit_bytes=VMEM_LIMIT),
        name="diff_attention",
    )(lam_params, subln_g, q, k, v)


def _outproj_kernel(x_ref, u_ref, o_ref, w_ref, g_ref, b_ref, wr_ref, br_ref,
                    x1_ref, ri_ref, rw_ref, cnt_ref, carry_scr, *, alpha, cw):
    acc = jnp.dot(u_ref[...], w_ref[0:cw, :], preferred_element_type=F32)
    acc = acc + jnp.dot(o_ref[...], w_ref[cw:, :], preferred_element_type=F32)
    x1 = _layernorm_rows(alpha * x_ref[...] + acc, g_ref[...], b_ref[...])
    x1_ref[...] = x1

    x_hi = x1.astype(BF16)
    x_lo = (x1 - x_hi.astype(F32)).astype(BF16)
    hi = jnp.dot(x_hi, wr_ref[...], preferred_element_type=F32)
    logits = (hi[:, :LANES] + hi[:, LANES:] + jnp.dot(x_lo, wr_ref[:, :LANES], preferred_element_type=F32)
              + br_ref[...])
    lane = lax.broadcasted_iota(jnp.int32, logits.shape, 1)
    neg = -jnp.inf
    gmask = lane < N_GROUPS
    lg = jnp.where(gmask, logits, neg)
    gmax = jnp.max(lg, axis=-1, keepdims=True)
    lane_f = lane.astype(F32)
    first = lambda hit: jnp.min(jnp.where(hit, lane_f, float(LANES)), axis=-1, keepdims=True).astype(jnp.int32)
    grp = first(lg == gmax)
    gsum = jnp.sum(jnp.where(gmask, jnp.exp(lg - gmax), 0.0), axis=-1, keepdims=True)
    gate = 1.0 / gsum
    emask = (lane >= N_GROUPS) & (lane < N_GROUPS + N_EXPERTS) & (((lane - N_GROUPS) // EXPERTS_PER_GROUP) == grp)
    ev = jnp.where(emask, logits, neg)
    v0 = jnp.max(ev, axis=-1, keepdims=True)
    i0 = first(emask & (ev == v0))
    emask1 = emask & (lane != i0)
    ev1 = jnp.where(emask1, logits, neg)
    v1 = jnp.max(ev1, axis=-1, keepdims=True)
    i1 = first(emask1 & (ev1 == v1))
    e1 = jnp.exp(v1 - v0)
    w0 = gate * (1.0 / (1.0 + e1))
    w1 = gate * (e1 / (1.0 + e1))

    @pl.when(pl.program_id(0) == 0)
    def _():
        carry_scr[...] = jnp.zeros(carry_scr.shape, F32)

    tm = logits.shape[0]
    oh0 = (lane == i0 - N_GROUPS).astype(F32)
    oh1 = (lane == i1 - N_GROUPS).astype(F32)
    both = oh0 + oh1
    tri = (lax.broadcasted_iota(jnp.int32, (tm, tm), 0) > lax.broadcasted_iota(jnp.int32, (tm, tm), 1)).astype(BF16)
    before = jnp.dot(tri, both.astype(BF16), preferred_element_type=F32) + carry_scr[...]
    rank0 = jnp.sum(oh0 * before, axis=-1, keepdims=True).astype(jnp.int32)
    rank1 = jnp.sum(oh1 * before, axis=-1, keepdims=True).astype(jnp.int32)
    carry_scr[...] = carry_scr[...] + jnp.sum(both, axis=0, keepdims=True)
    cnt_ref[...] = carry_scr[...]

    ri_ref[...] = jnp.where(lane == 0, i0 - N_GROUPS, jnp.where(lane == 1, i1 - N_GROUPS,
                            jnp.where(lane == 2, rank0, jnp.where(lane == 3, rank1, 0))))
    rw_ref[...] = jnp.where(lane == 0, w0, jnp.where(lane == 1, w1, 0.0))


def _outproj_router(x2, u2, o, w_bf, g, b, wr, br, *, tm, alpha):
    t, d = x2.shape
    cw = u2.shape[1]
    row = lambda i: (i, 0)
    const = lambda i: (0, 0)
    return pl.pallas_call(
        functools.partial(_outproj_kernel, alpha=alpha, cw=cw),
        grid=(t // tm,),
        in_specs=[
            pl.BlockSpec((tm, d), row),
            pl.BlockSpec((tm, cw), row),
            pl.BlockSpec((tm, o.shape[1]), row),
            pl.BlockSpec(w_bf.shape, const),
            pl.BlockSpec((1, d), const),
            pl.BlockSpec((1, d), const),
            pl.BlockSpec((d, 2 * LANES), const),
            pl.BlockSpec((1, LANES), const),
        ],
        out_specs=[
            pl.BlockSpec((tm, d), row),
            pl.BlockSpec((tm, LANES), row),
            pl.BlockSpec((tm, LANES), row),
            pl.BlockSpec((1, LANES), const),
        ],
        out_shape=[
            jax.ShapeDtypeStruct((t, d), F32),
            jax.ShapeDtypeStruct((t, LANES), jnp.int32),
            jax.ShapeDtypeStruct((t, LANES), F32),
            jax.ShapeDtypeStruct((1, LANES), F32),
        ],
        scratch_shapes=[pltpu.VMEM((1, LANES), F32)],
        compiler_params=pltpu.CompilerParams(dimension_semantics=("arbitrary",), vmem_limit_bytes=VMEM_LIMIT),
        name="outproj_router",
    )(x2, u2, o, w_bf, g, b, wr, br)


def _dispatch_kernel(pos_ref, pend_ref, x_ref, xs_hbm, zbuf, sem, zsem, *, tm):
    i = pl.program_id(0)

    def zero_copy(e):
        start = pl.multiple_of(pend_ref[e + 1] - MOE_BLOCK, MOE_BLOCK)
        return pltpu.make_async_copy(zbuf, xs_hbm.at[pl.ds(start, MOE_BLOCK)], zsem)

    def zero_block(b):
        return pltpu.make_async_copy(zbuf, xs_hbm.at[pl.ds(pl.multiple_of(b * MOE_BLOCK, MOE_BLOCK), MOE_BLOCK)], zsem)

    @pl.when(i == 0)
    def _():
        zbuf[...] = jnp.zeros(zbuf.shape, zbuf.dtype)
        first_unused = pend_ref[N_EXPERTS] // MOE_BLOCK
        n_blocks = xs_hbm.shape[0] // MOE_BLOCK
        for e in range(N_EXPERTS):
            @pl.when(pend_ref[e + 1] > pend_ref[e])
            def _():
                zero_copy(e).start()

        def start_unused(b, c):
            zero_block(b).start()
            return c

        def wait_unused(b, c):
            zero_block(b).wait()
            return c

        lax.fori_loop(first_unused, n_blocks, start_unused, 0)
        for e in range(N_EXPERTS):
            @pl.when(pend_ref[e + 1] > pend_ref[e])
            def _():
                zero_copy(e).wait()
        lax.fori_loop(first_unused, n_blocks, wait_unused, 0)

    def row_copy(r, dst_row):
        return pltpu.make_async_copy(x_ref.at[pl.ds(r % tm, 1)], xs_hbm.at[pl.ds(dst_row, 1)], sem)

    for r in range(2 * tm):
        row_copy(r, pos_ref[i * 2 * tm + r]).start(priority=r % 2)
    for r in range(2 * tm):
        row_copy(r, 0).wait()


def _dispatch(pos, pad_ends, x1, *, n_rows, tm):
    t, d = x1.shape
    grid_spec = pltpu.PrefetchScalarGridSpec(
        num_scalar_prefetch=2,
        grid=(t // tm,),
        in_specs=[pl.BlockSpec((tm, d), lambda i, pos, pe: (i, 0))],
        out_specs=pl.BlockSpec(memory_space=pl.ANY),
        scratch_shapes=[pltpu.VMEM((MOE_BLOCK, d), x1.dtype), pltpu.SemaphoreType.DMA(()), pltpu.SemaphoreType.DMA(())],
    )
    return pl.pallas_call(
        functools.partial(_dispatch_kernel, tm=tm),
        grid_spec=grid_spec,
        out_shape=jax.ShapeDtypeStruct((n_rows, d), x1.dtype),
        compiler_params=pltpu.CompilerParams(dimension_semantics=("arbitrary",)),
        name="moe_dispatch",
    )(pos, pad_ends, x1)


def _expert_kernel(blk_e_ref, nused_ref, xs_ref, wg_hbm, wu_hbm, wd_hbm, y_ref,
                   wg_f32, wu_f32, wd_f32, wg_bf, wu_bf, wd_bf, sem, slot_scr, *, layer):
    i = pl.program_id(0)
    n_used = nused_ref[0]
    last_block = pl.num_programs(0) - 1

    def fetch(e, s):
        return [pltpu.make_async_copy(src.at[layer, e], dst.at[s], sem.at[s])
                for src, dst in ((wg_hbm, wg_f32), (wu_hbm, wu_f32), (wd_hbm, wd_f32))]

    @pl.when(i < n_used)
    def _():
        @pl.when((i == 0) | (blk_e_ref[i] != blk_e_ref[jnp.maximum(i - 1, 0)]))
        def _():
            e = blk_e_ref[i]

            @pl.when(i == 0)
            def _():
                slot_scr[0] = 0
                for c in fetch(e, 0):
                    c.start()

            s = slot_scr[0]
            for c in fetch(e, s):
                c.wait()

            nxt = lax.while_loop(lambda j: (j < n_used) & (blk_e_ref[jnp.minimum(j, last_block)] == e),
                                 lambda j: j + 1, i + 1)

            @pl.when(nxt < n_used)
            def _():
                for c in fetch(blk_e_ref[jnp.minimum(nxt, last_block)], 1 - s):
                    c.start()

            slot_scr[0] = 1 - s
            wg_bf[...] = wg_f32[s].astype(BF16)
            wu_bf[...] = wu_f32[s].astype(BF16)
            wd_bf[...] = wd_f32[s].astype(BF16)

        xb = xs_ref[...].astype(BF16)
        gate = jnp.dot(xb, wg_bf[...], preferred_element_type=F32)
        up = jnp.dot(xb, wu_bf[...], preferred_element_type=F32)
        h = (gate * jax.nn.sigmoid(gate)) * up
        y_ref[...] = jnp.dot(h.astype(BF16), wd_bf[...], preferred_element_type=F32)

    @pl.when(i >= nused_ref[0])
    def _():
        y_ref[...] = jnp.zeros(y_ref.shape, y_ref.dtype)


def _expert_mlp(blk_e, n_used, xs, wg, wu, wd, *, layer):
    n_rows, d = xs.shape
    n_blocks = n_rows // MOE_BLOCK
    de = wg.shape[3]
    live = lambda i, be, nu: jnp.maximum(jnp.minimum(i, nu[0] - 1), 0)
    hbm = pl.BlockSpec(memory_space=pl.ANY)
    grid_spec = pltpu.PrefetchScalarGridSpec(
        num_scalar_prefetch=2,
        grid=(n_blocks,),
        in_specs=[pl.BlockSpec((MOE_BLOCK, d), lambda i, be, nu: (live(i, be, nu), 0)), hbm, hbm, hbm],
        out_specs=pl.BlockSpec((MOE_BLOCK, d), lambda i, be, nu: (i, 0)),
        scratch_shapes=[
            pltpu.VMEM((2, d, de), F32), pltpu.VMEM((2, d, de), F32), pltpu.VMEM((2, de, d), F32),
            pltpu.VMEM((d, de), BF16), pltpu.VMEM((d, de), BF16), pltpu.VMEM((de, d), BF16),
            pltpu.SemaphoreType.DMA((2,)),
            pltpu.SMEM((1,), jnp.int32),
        ],
    )
    return pl.pallas_call(
        functools.partial(_expert_kernel, layer=layer),
        grid_spec=grid_spec,
        out_shape=jax.ShapeDtypeStruct((n_rows, d), F32),
        compiler_params=pltpu.CompilerParams(dimension_semantics=("arbitrary",), vmem_limit_bytes=VMEM_LIMIT),
        name="expert_mlp",
    )(blk_e, n_used, xs, wg, wu, wd)


def _row_gather_copy(src_hbm, dst_buf, sem, slot, row, src_row):
    return pltpu.make_async_copy(src_hbm.at[pl.ds(src_row, 1)], dst_buf.at[slot, pl.ds(row, 1)], sem.at[slot])


def _combine_kernel(pos_ref, y_hbm, x_ref, rw_ref, g_ref, b_ref, o_ref, ybuf, sem, *, tm, alpha):
    i = pl.program_id(0)
    n = pl.num_programs(0)
    slot = i % 2

    def issue(tile, s):
        for r in range(2 * tm):
            _row_gather_copy(y_hbm, ybuf, sem, s, r, pos_ref[tile * 2 * tm + r]).start(priority=r % 2)

    def drain(s):
        for r in range(2 * tm):
            _row_gather_copy(y_hbm, ybuf, sem, s, r, 0).wait()

    @pl.when(i == 0)
    def _():
        issue(0, 0)
        issue(jnp.minimum(1, n - 1), 1)

    drain(slot)
    yy = ybuf[slot]
    rw = rw_ref[...]
    f = yy[:tm] * rw[:, 0:1] + yy[tm:] * rw[:, 1:2]
    o_ref[...] = _layernorm_rows(alpha * x_ref[...] + f, g_ref[...], b_ref[...])
    issue(jnp.minimum(i + 2, n - 1), slot)

    @pl.when(i == n - 1)
    def _():
        drain(slot)
        drain(1 - slot)


def _combine(pos, y, x1, route_w, g, b, *, tm, alpha):
    t, d = x1.shape
    grid_spec = pltpu.PrefetchScalarGridSpec(
        num_scalar_prefetch=1,
        grid=(t // tm,),
        in_specs=[
            pl.BlockSpec(memory_space=pl.ANY),
            pl.BlockSpec((tm, d), lambda i, pos: (i, 0)),
            pl.BlockSpec((tm, LANES), lambda i, pos: (i, 0)),
            pl.BlockSpec((1, d), lambda i, pos: (0, 0)),
            pl.BlockSpec((1, d), lambda i, pos: (0, 0)),
        ],
        out_specs=pl.BlockSpec((tm, d), lambda i, pos: (i, 0)),
        scratch_shapes=[pltpu.VMEM((2, 2 * tm, d), F32), pltpu.SemaphoreType.DMA((2,))],
    )
    return pl.pallas_call(
        functools.partial(_combine_kernel, tm=tm, alpha=alpha),
        grid_spec=grid_spec,
        out_shape=jax.ShapeDtypeStruct((t, d), F32),
        compiler_params=pltpu.CompilerParams(dimension_semantics=("arbitrary",), vmem_limit_bytes=VMEM_LIMIT),
        name="moe_combine",
    )(pos, y, x1, route_w, g, b)


def _tile_major(dest, tm):
    return dest.reshape(2, dest.shape[1] // tm, tm).transpose(1, 0, 2).reshape(-1)


def _dispatch_plan(route_i, counts):
    t = route_i.shape[0]
    counts = counts[0, :N_EXPERTS].astype(jnp.int32)
    padded = ((counts + MOE_BLOCK - 1) // MOE_BLOCK) * MOE_BLOCK
    pad_end = jnp.cumsum(padded)
    pad_start = pad_end - padded
    choice = route_i[:, 0:4].T
    dest = choice[2:4]
    for e in range(N_EXPERTS):
        dest = dest + jnp.where(choice[0:2] == e, pad_start[e], 0)
    n_blocks = -(-(2 * t) // MOE_BLOCK) + N_EXPERTS
    block_start = jnp.arange(n_blocks, dtype=jnp.int32) * MOE_BLOCK
    blk_e = jnp.minimum(jnp.sum((pad_end[None, :] <= block_start[:, None]).astype(jnp.int32), axis=1), N_EXPERTS - 1)
    pad_ends = jnp.concatenate([jnp.zeros((1,), jnp.int32), pad_end.astype(jnp.int32)])
    n_used = (pad_end[-1:] // MOE_BLOCK).astype(jnp.int32)
    return dest.astype(jnp.int32), pad_ends, blk_e, n_used, n_blocks * MOE_BLOCK


def _rope_tables(seq):
    half = HEAD_DIM // 2
    inv_freq = 1.0 / (ROPE_THETA ** (jnp.arange(half, dtype=F32) * 2.0 / HEAD_DIM))
    ang = jnp.arange(seq, dtype=F32)[:, None] * inv_freq[None, :]
    cos, sin = jnp.cos(ang), jnp.sin(ang)
    reps = LANES // HEAD_DIM
    return jnp.tile(jnp.concatenate([cos, cos], axis=1), (1, reps)), jnp.tile(jnp.concatenate([-sin, sin], axis=1), (1, reps))


def _pick_tile(n, want):
    while n % want:
        want //= 2
    return want


def kernel(x, w_in, b_in, conv_w, conv_b, conv_ln_g, conv_ln_b, lam_q1, lam_k1, lam_q2, lam_k2, subln_g, w_out, ln1_g, ln1_b, w_rg, b_rg, w_re, b_re, w_gate_e, w_up_e, w_down_e, ln2_g, ln2_b):
    batch, seq, d = x.shape
    depth = w_in.shape[0]
    t = batch * seq
    alpha = (2.0 * depth) ** 0.25
    cos_t, sin_t = _rope_tables(seq)
    tm = _pick_tile(seq, 512)
    tq = _pick_tile(seq, 512)
    tc = _pick_tile(seq, 256)
    td = _pick_tile(seq, 1024)
    row = lambda a: a.reshape(1, -1)

    x2 = x.reshape(t, d)
    for l in range(depth):
        lam_init = 0.8 - 0.6 * math.exp(-0.3 * l)
        u2, q, k, v = _inproj(x2, w_in[l].astype(BF16), row(b_in[l]), cos_t, sin_t,
                              conv_w[l].reshape(CONV_KERNEL, -1), row(conv_b[l]), row(conv_ln_g[l]),
                              row(conv_ln_b[l]), seq=seq, tm=tm)
        lam_params = jnp.stack([lam_q1[l], lam_k1[l], lam_q2[l], lam_k2[l]])
        o = _diff_attention(q, k, v, lam_params, row(subln_g[l]), batch=batch, seq=seq, tq=tq, lam_init=lam_init)
        w_router = jnp.concatenate([w_rg[l], w_re[l]], axis=1)
        w_router = jnp.pad(w_router, ((0, 0), (0, LANES - w_router.shape[1])))
        w_router_hi = w_router.astype(BF16)
        w_router = jnp.concatenate([w_router_hi, (w_router - w_router_hi.astype(F32)).astype(BF16)], axis=1)
        b_router = jnp.pad(jnp.concatenate([b_rg[l], b_re[l]]), (0, LANES - N_GROUPS - N_EXPERTS))
        x1, route_i, route_w, counts = _outproj_router(x2, u2, o, w_out[l].astype(BF16), row(ln1_g[l]), row(ln1_b[l]),
                                                       w_router, row(b_router), tm=tm, alpha=alpha)
        dest, pad_ends, blk_e, n_used, n_rows = _dispatch_plan(route_i, counts)
        xs = _dispatch(_tile_major(dest, td), pad_ends, x1, n_rows=n_rows, tm=td)
        y = _expert_mlp(blk_e, n_used, xs, w_gate_e, w_up_e, w_down_e, layer=l)
        x2 = _combine(_tile_major(dest, tc), y, x1, route_w, row(ln2_g[l]), row(ln2_b[l]), tm=tc, alpha=alpha)
    return x2.reshape(batch, seq, d)
```

```python
import functools
import math

import jax
import jax.numpy as jnp
from jax import lax
from jax.experimental import pallas as pl
from jax.experimental.pallas import tpu as pltpu

F32 = jnp.float32
BF16 = jnp.bfloat16

CHUNK = 64
CONV_KERNEL = 31
HEADS = 4
HEAD_DIM = 64
VALUE_DIM = 2 * HEAD_DIM
ROPE_THETA = 10000.0
N_GROUPS = 4
EXPERTS_PER_GROUP = 8
N_EXPERTS = N_GROUPS * EXPERTS_PER_GROUP
MOE_BLOCK = 256
LN_EPS = 1e-5
LANES = 128
SUBLANES = 8
CONV_HALO = 32
VMEM_LIMIT = 56 * 1024 * 1024


def _layernorm_rows(y, g, b):
    mu = jnp.mean(y, axis=-1, keepdims=True)
    d = y - mu
    var = jnp.mean(d * d, axis=-1, keepdims=True)
    return d * lax.rsqrt(var + LN_EPS) * g + b


def _inproj_kernel(x_ref, w_ref, b_ref, cos_ref, sin_ref, cw_ref, cb_ref, cg_ref, cbeta_ref,
                   u_ref, q_ref, k_ref, v_ref, buf_ref, sh_ref, wtab_ref, *, cw, qk, tiles_per_seq, rb):
    i = pl.program_id(0)
    tm = x_ref.shape[0]
    xb = x_ref[...].astype(BF16)

    def proj(c0, n):
        return jnp.dot(xb, w_ref[:, c0:c0 + n], preferred_element_type=F32) + b_ref[:, c0:c0 + n]

    @pl.when(i % tiles_per_seq == 0)
    def _():
        buf_ref[0:CONV_HALO, :] = jnp.zeros((CONV_HALO, buf_ref.shape[1]), F32)

    @pl.when(i % tiles_per_seq != 0)
    def _():
        buf_ref[0:CONV_HALO, :] = buf_ref[tm:tm + CONV_HALO, :]

    ag = proj(0, 2 * cw)
    buf_ref[CONV_HALO:CONV_HALO + tm, :] = ag[:, :cw] * jax.nn.sigmoid(ag[:, cw:])

    first = CONV_HALO - (CONV_KERNEL - 1)
    span = tm + CONV_HALO - SUBLANES
    for p in range(1, SUBLANES):
        sh_ref[p - 1, 0:span, :] = buf_ref[p:p + span, :]

    @pl.when(i == 0)
    def _():
        for j in range(CONV_KERNEL):
            wtab_ref[j] = jnp.broadcast_to(cw_ref[j:j + 1, :], wtab_ref.shape[1:])

    bias = cb_ref[...]
    g = cg_ref[...]
    beta = cbeta_ref[...]
    for r0 in range(0, tm, rb):
        acc = jnp.broadcast_to(bias, (rb, bias.shape[1]))
        for j in range(CONV_KERNEL):
            p = (first + j) % SUBLANES
            a0 = r0 + first + j - p
            window = buf_ref[a0:a0 + rb, :] if p == 0 else sh_ref[p - 1, a0:a0 + rb, :]
            acc = acc + jnp.concatenate([wtab_ref[j]] * (rb // SUBLANES), axis=0) * window
        y = _layernorm_rows(acc, g, beta)
        u_ref[r0:r0 + rb, :] = (y * jax.nn.sigmoid(y)).astype(u_ref.dtype)

    cos = cos_ref[...]
    sin = sin_ref[...]
    lane = lax.broadcasted_iota(jnp.int32, cos.shape, 1)
    first_half = (lane % HEAD_DIM) < (HEAD_DIM // 2)

    def rope(z, scale):
        outs = []
        for c in range(0, qk, LANES):
            zz = z[:, c:c + LANES]
            swapped = jnp.where(first_half, pltpu.roll(zz, LANES - HEAD_DIM // 2, 1),
                                pltpu.roll(zz, HEAD_DIM // 2, 1))
            outs.append((zz * cos + swapped * sin) * scale)
        return jnp.concatenate(outs, axis=1)

    q_ref[...] = rope(proj(2 * cw, qk), HEAD_DIM ** -0.5 * math.log2(math.e)).astype(BF16)
    k_ref[...] = rope(proj(2 * cw + qk, qk), 1.0).astype(BF16)
    v_ref[...] = proj(2 * cw + 2 * qk, v_ref.shape[1]).astype(BF16)


def _inproj(x2, w_bf, b, cos_t, sin_t, conv_w, conv_b, conv_g, conv_beta, *, seq, tm, rb=32):
    t, d = x2.shape
    n = w_bf.shape[1]
    cw = d // 2
    qk = 2 * HEADS * HEAD_DIM
    vw = HEADS * VALUE_DIM
    tiles_per_seq = seq // tm
    const = lambda i: (0, 0)
    return pl.pallas_call(
        functools.partial(_inproj_kernel, cw=cw, qk=qk, tiles_per_seq=tiles_per_seq, rb=rb),
        grid=(t // tm,),
        in_specs=[
            pl.BlockSpec((tm, d), lambda i: (i, 0)),
            pl.BlockSpec((d, n), const),
            pl.BlockSpec((1, n), const),
            pl.BlockSpec((tm, LANES), lambda i: (i % tiles_per_seq, 0)),
            pl.BlockSpec((tm, LANES), lambda i: (i % tiles_per_seq, 0)),
            pl.BlockSpec((CONV_KERNEL, cw), const),
            pl.BlockSpec((1, cw), const),
            pl.BlockSpec((1, cw), const),
            pl.BlockSpec((1, cw), const),
        ],
        out_specs=[
            pl.BlockSpec((tm, cw), lambda i: (i, 0)),
            pl.BlockSpec((tm, qk), lambda i: (i, 0)),
            pl.BlockSpec((tm, qk), lambda i: (i, 0)),
            pl.BlockSpec((tm, vw), lambda i: (i, 0)),
        ],
        out_shape=[
            jax.ShapeDtypeStruct((t, cw), BF16),
            jax.ShapeDtypeStruct((t, qk), BF16),
            jax.ShapeDtypeStruct((t, qk), BF16),
            jax.ShapeDtypeStruct((t, vw), BF16),
        ],
        scratch_shapes=[pltpu.VMEM((CONV_HALO + tm, cw), F32), pltpu.VMEM((SUBLANES - 1, CONV_HALO + tm, cw), F32),
                        pltpu.VMEM((CONV_KERNEL, SUBLANES, cw), F32)],
        compiler_params=pltpu.CompilerParams(dimension_semantics=("arbitrary",), vmem_limit_bytes=VMEM_LIMIT),
        name="inproj_conv",
    )(x2, w_bf, b, cos_t, sin_t, conv_w, conv_b, conv_g, conv_beta)


def _attn_kernel(lam_ref, g_ref, q_ref, k_ref, v_ref, o_ref, qq_scr, s_scr, p_scr, a_scr, m_scr, l_scr, acc_scr,
                 *, tq, rc, nh, lam_init):
    qi = pl.program_id(2)
    heads = range(nh)
    for h in heads:
        q = q_ref[:, h * VALUE_DIM:(h + 1) * VALUE_DIM]
        lane = lax.broadcasted_iota(jnp.int32, q.shape, 1)
        zero = jnp.zeros_like(q)
        qq_scr[h, 0:tq, :] = jnp.where(lane < HEAD_DIM, q, zero)
        qq_scr[h, tq:, :] = jnp.where(lane >= HEAD_DIM, q, zero)

    m_scr[...] = jnp.full(m_scr.shape, -jnp.inf, F32)
    l_scr[...] = jnp.zeros(l_scr.shape, F32)
    acc_scr[...] = jnp.zeros(acc_scr.shape, F32)

    def kv_rows(j):
        return pl.ds(pl.multiple_of(j * tq, tq), tq)

    def head_cols(h):
        return slice(h * VALUE_DIM, (h + 1) * VALUE_DIM)

    def scores(j, slot):
        for h in heads:
            s_scr[h, slot] = lax.dot_general(qq_scr[h], k_ref[kv_rows(j), head_cols(h)], (((1,), (1,)), ((), ())),
                                             preferred_element_type=F32)

    def softmax(slot, masked):
        for r0 in range(0, 2 * tq, rc):
            for h in heads:
                s = s_scr[h, slot, r0:r0 + rc, :]
                if masked:
                    row = (lax.broadcasted_iota(jnp.int32, s.shape, 0) + r0) % tq
                    col = lax.broadcasted_iota(jnp.int32, s.shape, 1)
                    s = jnp.where((col // CHUNK) <= (row // CHUNK), s, -jnp.inf)
                m_prev = m_scr[h, r0:r0 + rc, :]
                m_new = jnp.maximum(m_prev, jnp.max(s, axis=-1, keepdims=True))
                alpha = jnp.exp2(m_prev - m_new)
                p = jnp.exp2(s - jnp.tile(m_new, (1, tq // LANES)))
                l_scr[h, r0:r0 + rc, :] = alpha * l_scr[h, r0:r0 + rc, :] + jnp.sum(p, axis=-1, keepdims=True)
                m_scr[h, r0:r0 + rc, :] = m_new
                a_scr[h, slot, r0:r0 + rc, :] = alpha
                p_scr[h, slot, r0:r0 + rc, :] = p.astype(BF16)

    def values(j, slot):
        for h in heads:
            acc_scr[h] = a_scr[h, slot] * acc_scr[h] + jnp.dot(p_scr[h, slot], v_ref[kv_rows(j), head_cols(h)],
                                                               preferred_element_type=F32)

    def step(i, par):
        scores(i + 1, 1 - par)
        softmax(par, False)
        values(i - 1, 1 - par)

    def finish(par):
        softmax(par, True)
        values(qi - 1, 1 - par)
        values(qi, par)

    @pl.when(qi == 0)
    def _():
        scores(0, 0)
        softmax(0, True)
        values(0, 0)

    @pl.when(qi > 0)
    def _():
        scores(0, 0)
        scores(1, 1)
        softmax(0, False)

    def body(i, carry):
        for par in (0, 1):
            @pl.when(i % 2 == par)
            def _():
                step(i, par)
        return carry

    lax.fori_loop(1, qi, body, 0)

    for par in (0, 1):
        @pl.when((qi > 0) & (qi % 2 == par))
        def _():
            finish(par)

    lp = lam_ref[...]
    lam = (jnp.exp(jnp.sum(lp[0:1] * lp[1:2], axis=-1, keepdims=True))
           - jnp.exp(jnp.sum(lp[2:3] * lp[3:4], axis=-1, keepdims=True)) + lam_init)
    for h in heads:
        o = acc_scr[h] / l_scr[h]
        o = o[:tq] - lam * o[tq:]
        ms = jnp.mean(o * o, axis=-1, keepdims=True)
        o_ref[:, head_cols(h)] = (o * lax.rsqrt(ms + LN_EPS) * g_ref[...] * (1.0 - lam_init)).astype(o_ref.dtype)


def _diff_attention(q, k, v, lam_params, subln_g, *, batch, seq, tq, lam_init, nh=1):
    t = q.shape[0]
    nq = seq // tq
    w = nh * VALUE_DIM
    return pl.pallas_call(
        functools.partial(_attn_kernel, tq=tq, rc=min(256, tq), nh=nh, lam_init=lam_init),
        grid=(batch, HEADS // nh, nq),
        in_specs=[
            pl.BlockSpec((4, HEAD_DIM), lambda b, h, i: (0, 0)),
            pl.BlockSpec((1, VALUE_DIM), lambda b, h, i: (0, 0)),
            pl.BlockSpec((tq, w), lambda b, h, i: (b * nq + i, h)),
            pl.BlockSpec((seq, w), lambda b, h, i: (b, h)),
            pl.BlockSpec((seq, w), lambda b, h, i: (b, h)),
        ],
        out_specs=pl.BlockSpec((tq, w), lambda b, h, i: (b * nq + i, h)),
        out_shape=jax.ShapeDtypeStruct((t, HEADS * VALUE_DIM), BF16),
        scratch_shapes=[
            pltpu.VMEM((nh, 2 * tq, VALUE_DIM), BF16),
            pltpu.VMEM((nh, 2, 2 * tq, tq), F32),
            pltpu.VMEM((nh, 2, 2 * tq, tq), BF16),
            pltpu.VMEM((nh, 2, 2 * tq, LANES), F32),
            pltpu.VMEM((nh, 2 * tq, LANES), F32),
            pltpu.VMEM((nh, 2 * tq, LANES), F32),
            pltpu.VMEM((nh, 2 * tq, VALUE_DIM), F32),
        ],
        compiler_params=pltpu.CompilerParams(dimension_semantics=("arbitrary", "arbitrary", "arbitrary"),
                                             vmem_limit_bytes=VMEM_LIMIT),
        name="diff_attention",
    )(lam_params, subln_g, q, k, v)


def _outproj_kernel(x_ref, u_ref, o_ref, w_ref, g_ref, b_ref, wr_ref, br_ref,
                    x1_ref, ri_ref, rw_ref, cnt_ref, carry_scr, *, alpha, cw):
    acc = jnp.dot(u_ref[...], w_ref[0:cw, :], preferred_element_type=F32)
    acc = acc + jnp.dot(o_ref[...], w_ref[cw:, :], preferred_element_type=F32)
    x1 = _layernorm_rows(alpha * x_ref[...] + acc, g_ref[...], b_ref[...])
    x1_ref[...] = x1

    x_hi = x1.astype(BF16)
    x_lo = (x1 - x_hi.astype(F32)).astype(BF16)
    hi = jnp.dot(x_hi, wr_ref[...], preferred_element_type=F32)
    logits = (hi[:, :LANES] + hi[:, LANES:] + jnp.dot(x_lo, wr_ref[:, :LANES], preferred_element_type=F32)
              + br_ref[...])
    lane = lax.broadcasted_iota(jnp.int32, logits.shape, 1)
    neg = -jnp.inf
    gmask = lane < N_GROUPS
    lg = jnp.where(gmask, logits, neg)
    gmax = jnp.max(lg, axis=-1, keepdims=True)
    lane_f = lane.astype(F32)
    first = lambda hit: jnp.min(jnp.where(hit, lane_f, float(LANES)), axis=-1, keepdims=True).astype(jnp.int32)
    grp = first(lg == gmax)
    gsum = jnp.sum(jnp.where(gmask, jnp.exp(lg - gmax), 0.0), axis=-1, keepdims=True)
    gate = 1.0 / gsum
    emask = (lane >= N_GROUPS) & (lane < N_GROUPS + N_EXPERTS) & (((lane - N_GROUPS) // EXPERTS_PER_GROUP) == grp)
    ev = jnp.where(emask, logits, neg)
    v0 = jnp.max(ev, axis=-1, keepdims=True)
    i0 = first(emask & (ev == v0))
    emask1 = emask & (lane != i0)
    ev1 = jnp.where(emask1, logits, neg)
    v1 = jnp.max(ev1, axis=-1, keepdims=True)
    i1 = first(emask1 & (ev1 == v1))
    e1 = jnp.exp(v1 - v0)
    w0 = gate * (1.0 / (1.0 + e1))
    w1 = gate * (e1 / (1.0 + e1))

    @pl.when(pl.program_id(0) == 0)
    def _():
        carry_scr[...] = jnp.zeros(carry_scr.shape, F32)

    tm = logits.shape[0]
    oh0 = (lane == i0 - N_GROUPS).astype(F32)
    oh1 = (lane == i1 - N_GROUPS).astype(F32)
    both = oh0 + oh1
    tri = (lax.broadcasted_iota(jnp.int32, (tm, tm), 0) > lax.broadcasted_iota(jnp.int32, (tm, tm), 1)).astype(BF16)
    before = jnp.dot(tri, both.astype(BF16), preferred_element_type=F32) + carry_scr[...]
    rank0 = jnp.sum(oh0 * before, axis=-1, keepdims=True)
    rank1 = jnp.sum(oh1 * before, axis=-1, keepdims=True)
    carry_scr[...] = carry_scr[...] + jnp.sum(both, axis=0, keepdims=True)
    cnt_ref[...] = carry_scr[...]

    choice = jnp.where(lane == 0, (i0 - N_GROUPS).astype(F32), jnp.where(lane == 1, (i1 - N_GROUPS).astype(F32),
                       jnp.where(lane == 2, rank0, jnp.where(lane == 3, rank1, 0.0))))
    ri_ref[...] = jnp.transpose(choice)[0:SUBLANES, :]
    rw_ref[...] = jnp.where(lane == 0, w0, jnp.where(lane == 1, w1, 0.0))


def _outproj_router(x2, u2, o, w_bf, g, b, wr, br, *, tm, alpha):
    t, d = x2.shape
    cw = u2.shape[1]
    row = lambda i: (i, 0)
    const = lambda i: (0, 0)
    return pl.pallas_call(
        functools.partial(_outproj_kernel, alpha=alpha, cw=cw),
        grid=(t // tm,),
        in_specs=[
            pl.BlockSpec((tm, d), row),
            pl.BlockSpec((tm, cw), row),
            pl.BlockSpec((tm, o.shape[1]), row),
            pl.BlockSpec(w_bf.shape, const),
            pl.BlockSpec((1, d), const),
            pl.BlockSpec((1, d), const),
            pl.BlockSpec((d, 2 * LANES), const),
            pl.BlockSpec((1, LANES), const),
        ],
        out_specs=[
            pl.BlockSpec((tm, d), row),
            pl.BlockSpec((SUBLANES, tm), lambda i: (0, i)),
            pl.BlockSpec((tm, LANES), row),
            pl.BlockSpec((1, LANES), const),
        ],
        out_shape=[
            jax.ShapeDtypeStruct((t, d), F32),
            jax.ShapeDtypeStruct((SUBLANES, t), F32),
            jax.ShapeDtypeStruct((t, LANES), F32),
            jax.ShapeDtypeStruct((1, LANES), F32),
        ],
        scratch_shapes=[pltpu.VMEM((1, LANES), F32)],
        compiler_params=pltpu.CompilerParams(dimension_semantics=("arbitrary",), vmem_limit_bytes=VMEM_LIMIT),
        name="outproj_router",
    )(x2, u2, o, w_bf, g, b, wr, br)


def _dispatch_kernel(pos_ref, pend_ref, x_ref, xs_hbm, zbuf, sem, zsem, *, tm):
    i = pl.program_id(0)

    def zero_copy(e):
        start = pl.multiple_of(pend_ref[e + 1] - MOE_BLOCK, MOE_BLOCK)
        return pltpu.make_async_copy(zbuf, xs_hbm.at[pl.ds(start, MOE_BLOCK)], zsem)

    def zero_block(b):
        return pltpu.make_async_copy(zbuf, xs_hbm.at[pl.ds(pl.multiple_of(b * MOE_BLOCK, MOE_BLOCK), MOE_BLOCK)], zsem)

    @pl.when(i == 0)
    def _():
        zbuf[...] = jnp.zeros(zbuf.shape, zbuf.dtype)
        first_unused = pend_ref[N_EXPERTS] // MOE_BLOCK
        n_blocks = xs_hbm.shape[0] // MOE_BLOCK
        for e in range(N_EXPERTS):
            @pl.when(pend_ref[e + 1] > pend_ref[e])
            def _():
                zero_copy(e).start()

        def start_unused(b, c):
            zero_block(b).start()
            return c

        def wait_unused(b, c):
            zero_block(b).wait()
            return c

        lax.fori_loop(first_unused, n_blocks, start_unused, 0)
        for e in range(N_EXPERTS):
            @pl.when(pend_ref[e + 1] > pend_ref[e])
            def _():
                zero_copy(e).wait()
        lax.fori_loop(first_unused, n_blocks, wait_unused, 0)

    def row_copy(r, dst_row):
        return pltpu.make_async_copy(x_ref.at[pl.ds(r % tm, 1)], xs_hbm.at[pl.ds(dst_row, 1)], sem)

    for r in range(2 * tm):
        row_copy(r, pos_ref[i * 2 * tm + r]).start(priority=r % 2)
    for r in range(2 * tm):
        row_copy(r, 0).wait()


def _dispatch(pos, pad_ends, x1, *, n_rows, tm):
    t, d = x1.shape
    grid_spec = pltpu.PrefetchScalarGridSpec(
        num_scalar_prefetch=2,
        grid=(t // tm,),
        in_specs=[pl.BlockSpec((tm, d), lambda i, pos, pe: (i, 0))],
        out_specs=pl.BlockSpec(memory_space=pl.ANY),
        scratch_shapes=[pltpu.VMEM((MOE_BLOCK, d), x1.dtype), pltpu.SemaphoreType.DMA(()), pltpu.SemaphoreType.DMA(())],
    )
    return pl.pallas_call(
        functools.partial(_dispatch_kernel, tm=tm),
        grid_spec=grid_spec,
        out_shape=jax.ShapeDtypeStruct((n_rows, d), x1.dtype),
        compiler_params=pltpu.CompilerParams(dimension_semantics=("arbitrary",)),
        name="moe_dispatch",
    )(pos, pad_ends, x1)


def _expert_kernel(blk_e_ref, nused_ref, xs_ref, wg_hbm, wu_hbm, wd_hbm, y_ref,
                   wg_f32, wu_f32, wd_f32, wg_bf, wu_bf, wd_bf, sem, slot_scr, *, layer):
    i = pl.program_id(0)
    n_used = nused_ref[0]
    last_block = pl.num_programs(0) - 1

    def fetch(e, s):
        return [pltpu.make_async_copy(src.at[layer, e], dst.at[s], sem.at[s])
                for src, dst in ((wg_hbm, wg_f32), (wu_hbm, wu_f32), (wd_hbm, wd_f32))]

    @pl.when(i < n_used)
    def _():
        @pl.when((i == 0) | (blk_e_ref[i] != blk_e_ref[jnp.maximum(i - 1, 0)]))
        def _():
            e = blk_e_ref[i]

            @pl.when(i == 0)
            def _():
                slot_scr[0] = 0
                for c in fetch(e, 0):
                    c.start()

            s = slot_scr[0]
            for c in fetch(e, s):
                c.wait()

            nxt = lax.while_loop(lambda j: (j < n_used) & (blk_e_ref[jnp.minimum(j, last_block)] == e),
                                 lambda j: j + 1, i + 1)

            @pl.when(nxt < n_used)
            def _():
                for c in fetch(blk_e_ref[jnp.minimum(nxt, last_block)], 1 - s):
                    c.start()

            slot_scr[0] = 1 - s
            wg_bf[...] = wg_f32[s].astype(BF16)
            wu_bf[...] = wu_f32[s].astype(BF16)
            wd_bf[...] = wd_f32[s].astype(BF16)

        xb = xs_ref[...].astype(BF16)
        gate = jnp.dot(xb, wg_bf[...], preferred_element_type=F32)
        up = jnp.dot(xb, wu_bf[...], preferred_element_type=F32)
        h = (gate * jax.nn.sigmoid(gate)) * up
        y_ref[...] = jnp.dot(h.astype(BF16), wd_bf[...], preferred_element_type=F32)

    @pl.when(i >= nused_ref[0])
    def _():
        y_ref[...] = jnp.zeros(y_ref.shape, y_ref.dtype)


def _expert_mlp(blk_e, n_used, xs, wg, wu, wd, *, layer):
    n_rows, d = xs.shape
    n_blocks = n_rows // MOE_BLOCK
    de = wg.shape[3]
    live = lambda i, be, nu: jnp.maximum(jnp.minimum(i, nu[0] - 1), 0)
    hbm = pl.BlockSpec(memory_space=pl.ANY)
    grid_spec = pltpu.PrefetchScalarGridSpec(
        num_scalar_prefetch=2,
        grid=(n_blocks,),
        in_specs=[pl.BlockSpec((MOE_BLOCK, d), lambda i, be, nu: (live(i, be, nu), 0)), hbm, hbm, hbm],
        out_specs=pl.BlockSpec((MOE_BLOCK, d), lambda i, be, nu: (i, 0)),
        scratch_shapes=[
            pltpu.VMEM((2, d, de), F32), pltpu.VMEM((2, d, de), F32), pltpu.VMEM((2, de, d), F32),
            pltpu.VMEM((d, de), BF16), pltpu.VMEM((d, de), BF16), pltpu.VMEM((de, d), BF16),
            pltpu.SemaphoreType.DMA((2,)),
            pltpu.SMEM((1,), jnp.int32),
        ],
    )
    return pl.pallas_call(
        functools.partial(_expert_kernel, layer=layer),
        grid_spec=grid_spec,
        out_shape=jax.ShapeDtypeStruct((n_rows, d), F32),
        compiler_params=pltpu.CompilerParams(dimension_semantics=("arbitrary",), vmem_limit_bytes=VMEM_LIMIT),
        name="expert_mlp",
    )(blk_e, n_used, xs, wg, wu, wd)


def _row_gather_copy(src_hbm, dst_buf, sem, slot, row, src_row):
    return pltpu.make_async_copy(src_hbm.at[pl.ds(src_row, 1)], dst_buf.at[slot, pl.ds(row, 1)], sem.at[slot])


def _combine_kernel(pos_ref, y_hbm, x_ref, rw_ref, g_ref, b_ref, o_ref, ybuf, sem, *, tm, alpha):
    i = pl.program_id(0)
    n = pl.num_programs(0)
    slot = i % 2

    def issue(tile, s):
        for r in range(2 * tm):
            _row_gather_copy(y_hbm, ybuf, sem, s, r, pos_ref[tile * 2 * tm + r]).start(priority=r % 2)

    def drain(s):
        for r in range(2 * tm):
            _row_gather_copy(y_hbm, ybuf, sem, s, r, 0).wait()

    @pl.when(i == 0)
    def _():
        issue(0, 0)
        issue(jnp.minimum(1, n - 1), 1)

    drain(slot)
    yy = ybuf[slot]
    rw = rw_ref[...]
    f = yy[:tm] * rw[:, 0:1] + yy[tm:] * rw[:, 1:2]
    o_ref[...] = _layernorm_rows(alpha * x_ref[...] + f, g_ref[...], b_ref[...])
    issue(jnp.minimum(i + 2, n - 1), slot)

    @pl.when(i == n - 1)
    def _():
        drain(slot)
        drain(1 - slot)


def _combine(pos, y, x1, route_w, g, b, *, tm, alpha):
    t, d = x1.shape
    grid_spec = pltpu.PrefetchScalarGridSpec(
        num_scalar_prefetch=1,
        grid=(t // tm,),
        in_specs=[
            pl.BlockSpec(memory_space=pl.ANY),
            pl.BlockSpec((tm, d), lambda i, pos: (i, 0)),
            pl.BlockSpec((tm, LANES), lambda i, pos: (i, 0)),
            pl.BlockSpec((1, d), lambda i, pos: (0, 0)),
            pl.BlockSpec((1, d), lambda i, pos: (0, 0)),
        ],
        out_specs=pl.BlockSpec((tm, d), lambda i, pos: (i, 0)),
        scratch_shapes=[pltpu.VMEM((2, 2 * tm, d), F32), pltpu.SemaphoreType.DMA((2,))],
    )
    return pl.pallas_call(
        functools.partial(_combine_kernel, tm=tm, alpha=alpha),
        grid_spec=grid_spec,
        out_shape=jax.ShapeDtypeStruct((t, d), F32),
        compiler_params=pltpu.CompilerParams(dimension_semantics=("arbitrary",), vmem_limit_bytes=VMEM_LIMIT),
        name="moe_combine",
    )(pos, y, x1, route_w, g, b)


def _tile_major(dest, tm):
    return dest.reshape(2, dest.shape[1] // tm, tm).transpose(1, 0, 2).reshape(-1)


def _dispatch_plan(route_i, counts):
    t = route_i.shape[1]
    counts = counts[0, :N_EXPERTS].astype(jnp.int32)
    padded = ((counts + MOE_BLOCK - 1) // MOE_BLOCK) * MOE_BLOCK
    pad_end = jnp.cumsum(padded)
    pad_start = pad_end - padded
    choice = route_i[0:4].astype(jnp.int32)
    dest = choice[2:4]
    for e in range(N_EXPERTS):
        dest = dest + jnp.where(choice[0:2] == e, pad_start[e], 0)
    n_blocks = -(-(2 * t) // MOE_BLOCK) + N_EXPERTS
    block_start = jnp.arange(n_blocks, dtype=jnp.int32) * MOE_BLOCK
    blk_e = jnp.minimum(jnp.sum((pad_end[None, :] <= block_start[:, None]).astype(jnp.int32), axis=1), N_EXPERTS - 1)
    pad_ends = jnp.concatenate([jnp.zeros((1,), jnp.int32), pad_end.astype(jnp.int32)])
    n_used = (pad_end[-1:] // MOE_BLOCK).astype(jnp.int32)
    return dest.astype(jnp.int32), pad_ends, blk_e, n_used, n_blocks * MOE_BLOCK


def _rope_tables(seq):
    half = HEAD_DIM // 2
    inv_freq = 1.0 / (ROPE_THETA ** (jnp.arange(half, dtype=F32) * 2.0 / HEAD_DIM))
    ang = jnp.arange(seq, dtype=F32)[:, None] * inv_freq[None, :]
    cos, sin = jnp.cos(ang), jnp.sin(ang)
    reps = LANES // HEAD_DIM
    return jnp.tile(jnp.concatenate([cos, cos], axis=1), (1, reps)), jnp.tile(jnp.concatenate([-sin, sin], axis=1), (1, reps))


def _pick_tile(n, want):
    while n % want:
        want //= 2
    return want


def kernel(x, w_in, b_in, conv_w, conv_b, conv_ln_g, conv_ln_b, lam_q1, lam_k1, lam_q2, lam_k2, subln_g, w_out, ln1_g, ln1_b, w_rg, b_rg, w_re, b_re, w_gate_e, w_up_e, w_down_e, ln2_g, ln2_b):
    batch, seq, d = x.shape
    depth = w_in.shape[0]
    t = batch * seq
    alpha = (2.0 * depth) ** 0.25
    cos_t, sin_t = _rope_tables(seq)
    tm = _pick_tile(seq, 512)
    tq = _pick_tile(seq, 512)
    tc = _pick_tile(seq, 256)
    td = _pick_tile(seq, 1024)
    row = lambda a: a.reshape(1, -1)

    x2 = x.reshape(t, d)
    for l in range(depth):
        lam_init = 0.8 - 0.6 * math.exp(-0.3 * l)
        u2, q, k, v = _inproj(x2, w_in[l].astype(BF16), row(b_in[l]), cos_t, sin_t,
                              conv_w[l].reshape(CONV_KERNEL, -1), row(conv_b[l]), row(conv_ln_g[l]),
                              row(conv_ln_b[l]), seq=seq, tm=tm)
        lam_params = jnp.stack([lam_q1[l], lam_k1[l], lam_q2[l], lam_k2[l]])
        o = _diff_attention(q, k, v, lam_params, row(subln_g[l]), batch=batch, seq=seq, tq=tq, lam_init=lam_init)
        w_router = jnp.concatenate([w_rg[l], w_re[l]], axis=1)
        w_router = jnp.pad(w_router, ((0, 0), (0, LANES - w_router.shape[1])))
        w_router_hi = w_router.astype(BF16)
        w_router = jnp.concatenate([w_router_hi, (w_router - w_router_hi.astype(F32)).astype(BF16)], axis=1)
        b_router = jnp.pad(jnp.concatenate([b_rg[l], b_re[l]]), (0, LANES - N_GROUPS - N_EXPERTS))
        x1, route_i, route_w, counts = _outproj_router(x2, u2, o, w_out[l].astype(BF16), row(ln1_g[l]), row(ln1_b[l]),
                                                       w_router, row(b_router), tm=tm, alpha=alpha)
        dest, pad_ends, blk_e, n_used, n_rows = _dispatch_plan(route_i, counts)
        xs = _dispatch(_tile_major(dest, td), pad_ends, x1, n_rows=n_rows, tm=td)
        y = _expert_mlp(blk_e, n_used, xs, w_gate_e, w_up_e, w_down_e, layer=l)
        x2 = _combine(_tile_major(dest, tc), y, x1, route_w, row(ln2_g[l]), row(ln2_b[l]), tm=tc, alpha=alpha)
    return x2.reshape(batch, seq, d)
```

```python
import functools
import math

import jax
import jax.numpy as jnp
from jax import lax
from jax.experimental import pallas as pl
from jax.experimental.pallas import tpu as pltpu

F32 = jnp.float32
BF16 = jnp.bfloat16

CHUNK = 64
CONV_KERNEL = 31
HEADS = 4
HEAD_DIM = 64
VALUE_DIM = 2 * HEAD_DIM
ROPE_THETA = 10000.0
N_GROUPS = 4
EXPERTS_PER_GROUP = 8
N_EXPERTS = N_GROUPS * EXPERTS_PER_GROUP
MOE_BLOCK = 256
LN_EPS = 1e-5
LANES = 128
SUBLANES = 8
CONV_HALO = 32
VMEM_LIMIT = 56 * 1024 * 1024


def _layernorm_rows(y, g, b):
    mu = jnp.mean(y, axis=-1, keepdims=True)
    d = y - mu
    var = jnp.mean(d * d, axis=-1, keepdims=True)
    return d * lax.rsqrt(var + LN_EPS) * g + b


def _inproj_kernel(x_ref, w_ref, b_ref, cos_ref, sin_ref, cw_ref, cb_ref, cg_ref, cbeta_ref,
                   u_ref, q_ref, k_ref, v_ref, buf_ref, sh_ref, wtab_ref, *, cw, qk, tiles_per_seq, rb):
    i = pl.program_id(0)
    tm = x_ref.shape[0]
    xb = x_ref[...].astype(BF16)

    def proj(c0, n):
        return jnp.dot(xb, w_ref[:, c0:c0 + n], preferred_element_type=F32) + b_ref[:, c0:c0 + n]

    @pl.when(i % tiles_per_seq == 0)
    def _():
        buf_ref[0:CONV_HALO, :] = jnp.zeros((CONV_HALO, buf_ref.shape[1]), F32)

    @pl.when(i % tiles_per_seq != 0)
    def _():
        buf_ref[0:CONV_HALO, :] = buf_ref[tm:tm + CONV_HALO, :]

    ag = proj(0, 2 * cw)
    buf_ref[CONV_HALO:CONV_HALO + tm, :] = ag[:, :cw] * jax.nn.sigmoid(ag[:, cw:])

    first = CONV_HALO - (CONV_KERNEL - 1)
    span = tm + CONV_HALO - SUBLANES
    for p in range(1, SUBLANES):
        sh_ref[p - 1, 0:span, :] = buf_ref[p:p + span, :]

    @pl.when(i == 0)
    def _():
        for j in range(CONV_KERNEL):
            wtab_ref[j] = jnp.broadcast_to(cw_ref[j:j + 1, :], wtab_ref.shape[1:])

    bias = cb_ref[...]
    g = cg_ref[...]
    beta = cbeta_ref[...]
    for r0 in range(0, tm, rb):
        acc = jnp.broadcast_to(bias, (rb, bias.shape[1]))
        for j in range(CONV_KERNEL):
            p = (first + j) % SUBLANES
            a0 = r0 + first + j - p
            window = buf_ref[a0:a0 + rb, :] if p == 0 else sh_ref[p - 1, a0:a0 + rb, :]
            acc = acc + jnp.concatenate([wtab_ref[j]] * (rb // SUBLANES), axis=0) * window
        y = _layernorm_rows(acc, g, beta)
        u_ref[r0:r0 + rb, :] = (y * jax.nn.sigmoid(y)).astype(u_ref.dtype)

    cos = cos_ref[...]
    sin = sin_ref[...]
    lane = lax.broadcasted_iota(jnp.int32, cos.shape, 1)
    first_half = (lane % HEAD_DIM) < (HEAD_DIM // 2)

    def rope(z, scale):
        outs = []
        for c in range(0, qk, LANES):
            zz = z[:, c:c + LANES]
            swapped = jnp.where(first_half, pltpu.roll(zz, LANES - HEAD_DIM // 2, 1),
                                pltpu.roll(zz, HEAD_DIM // 2, 1))
            outs.append((zz * cos + swapped * sin) * scale)
        return jnp.concatenate(outs, axis=1)

    q_ref[...] = rope(proj(2 * cw, qk), HEAD_DIM ** -0.5 * math.log2(math.e)).astype(BF16)
    k_ref[...] = rope(proj(2 * cw + qk, qk), 1.0).astype(BF16)
    v_ref[...] = proj(2 * cw + 2 * qk, v_ref.shape[1]).astype(BF16)


def _inproj(x2, w_bf, b, cos_t, sin_t, conv_w, conv_b, conv_g, conv_beta, *, seq, tm, rb=32):
    t, d = x2.shape
    n = w_bf.shape[1]
    cw = d // 2
    qk = 2 * HEADS * HEAD_DIM
    vw = HEADS * VALUE_DIM
    tiles_per_seq = seq // tm
    const = lambda i: (0, 0)
    return pl.pallas_call(
        functools.partial(_inproj_kernel, cw=cw, qk=qk, tiles_per_seq=tiles_per_seq, rb=rb),
        grid=(t // tm,),
        in_specs=[
            pl.BlockSpec((tm, d), lambda i: (i, 0)),
            pl.BlockSpec((d, n), const),
            pl.BlockSpec((1, n), const),
            pl.BlockSpec((tm, LANES), lambda i: (i % tiles_per_seq, 0)),
            pl.BlockSpec((tm, LANES), lambda i: (i % tiles_per_seq, 0)),
            pl.BlockSpec((CONV_KERNEL, cw), const),
            pl.BlockSpec((1, cw), const),
            pl.BlockSpec((1, cw), const),
            pl.BlockSpec((1, cw), const),
        ],
        out_specs=[
            pl.BlockSpec((tm, cw), lambda i: (i, 0)),
            pl.BlockSpec((tm, qk), lambda i: (i, 0)),
            pl.BlockSpec((tm, qk), lambda i: (i, 0)),
            pl.BlockSpec((tm, vw), lambda i: (i, 0)),
        ],
        out_shape=[
            jax.ShapeDtypeStruct((t, cw), BF16),
            jax.ShapeDtypeStruct((t, qk), BF16),
            jax.ShapeDtypeStruct((t, qk), BF16),
            jax.ShapeDtypeStruct((t, vw), BF16),
        ],
        scratch_shapes=[pltpu.VMEM((CONV_HALO + tm, cw), F32), pltpu.VMEM((SUBLANES - 1, CONV_HALO + tm, cw), F32),
                        pltpu.VMEM((CONV_KERNEL, SUBLANES, cw), F32)],
        compiler_params=pltpu.CompilerParams(dimension_semantics=("arbitrary",), vmem_limit_bytes=VMEM_LIMIT),
        name="inproj_conv",
    )(x2, w_bf, b, cos_t, sin_t, conv_w, conv_b, conv_g, conv_beta)


def _attn_kernel(lam_ref, g_ref, q_ref, k_ref, v_ref, o_ref, qq_scr, s_scr, p_scr, a_scr, m_scr, l_scr, acc_scr,
                 *, tq, rc, lam_init):
    qi = pl.program_id(2)
    q = q_ref[...]
    lane = lax.broadcasted_iota(jnp.int32, q.shape, 1)
    zero = jnp.zeros_like(q)
    qq_scr[0:tq, :] = jnp.where(lane < HEAD_DIM, q, zero)
    qq_scr[tq:, :] = jnp.where(lane >= HEAD_DIM, q, zero)

    m_scr[...] = jnp.full(m_scr.shape, -jnp.inf, F32)
    l_scr[...] = jnp.zeros(l_scr.shape, F32)
    acc_scr[...] = jnp.zeros(acc_scr.shape, F32)

    def kv_rows(j):
        return pl.ds(pl.multiple_of(j * tq, tq), tq)

    def scores(j, slot):
        s_scr[slot] = lax.dot_general(qq_scr[...], k_ref[kv_rows(j), :], (((1,), (1,)), ((), ())),
                                      preferred_element_type=F32)

    def softmax(slot, masked):
        for r0 in range(0, 2 * tq, rc):
            s = s_scr[slot, r0:r0 + rc, :]
            if masked:
                row = (lax.broadcasted_iota(jnp.int32, s.shape, 0) + r0) % tq
                col = lax.broadcasted_iota(jnp.int32, s.shape, 1)
                s = jnp.where((col // CHUNK) <= (row // CHUNK), s, -jnp.inf)
            m_prev = m_scr[r0:r0 + rc, :]
            m_new = jnp.maximum(m_prev, jnp.max(s, axis=-1, keepdims=True))
            alpha = jnp.exp2(m_prev - m_new)
            p = jnp.exp2(s - jnp.tile(m_new, (1, tq // LANES)))
            l_scr[r0:r0 + rc, :] = alpha * l_scr[r0:r0 + rc, :] + jnp.sum(p, axis=-1, keepdims=True)
            m_scr[r0:r0 + rc, :] = m_new
            a_scr[slot, r0:r0 + rc, :] = alpha
            p_scr[slot, r0:r0 + rc, :] = p.astype(BF16)

    def values(j, slot):
        acc_scr[...] = a_scr[slot] * acc_scr[...] + jnp.dot(p_scr[slot], v_ref[kv_rows(j), :],
                                                            preferred_element_type=F32)

    def step(i, par):
        scores(i + 1, 1 - par)
        softmax(par, False)
        values(i - 1, 1 - par)

    def finish(par):
        softmax(par, True)
        values(qi - 1, 1 - par)
        values(qi, par)

    @pl.when(qi == 0)
    def _():
        scores(0, 0)
        softmax(0, True)
        values(0, 0)

    @pl.when(qi > 0)
    def _():
        scores(0, 0)
        scores(1, 1)
        softmax(0, False)

    def body(i, carry):
        for par in (0, 1):
            @pl.when(i % 2 == par)
            def _():
                step(i, par)
        return carry

    lax.fori_loop(1, qi, body, 0)

    for par in (0, 1):
        @pl.when((qi > 0) & (qi % 2 == par))
        def _():
            finish(par)

    lp = lam_ref[...]
    lam = (jnp.exp(jnp.sum(lp[0:1] * lp[1:2], axis=-1, keepdims=True))
           - jnp.exp(jnp.sum(lp[2:3] * lp[3:4], axis=-1, keepdims=True)) + lam_init)
    o = acc_scr[...] / l_scr[...]
    o = o[:tq] - lam * o[tq:]
    ms = jnp.mean(o * o, axis=-1, keepdims=True)
    o_ref[...] = (o * lax.rsqrt(ms + LN_EPS) * g_ref[...] * (1.0 - lam_init)).astype(o_ref.dtype)


def _diff_attention(q, k, v, lam_params, subln_g, *, batch, seq, tq, lam_init):
    t = q.shape[0]
    nq = seq // tq
    return pl.pallas_call(
        functools.partial(_attn_kernel, tq=tq, rc=min(256, tq), lam_init=lam_init),
        grid=(batch, HEADS, nq),
        in_specs=[
            pl.BlockSpec((4, HEAD_DIM), lambda b, h, i: (0, 0)),
            pl.BlockSpec((1, VALUE_DIM), lambda b, h, i: (0, 0)),
            pl.BlockSpec((tq, VALUE_DIM), lambda b, h, i: (b * nq + i, h)),
            pl.BlockSpec((seq, VALUE_DIM), lambda b, h, i: (b, h)),
            pl.BlockSpec((seq, VALUE_DIM), lambda b, h, i: (b, h)),
        ],
        out_specs=pl.BlockSpec((tq, VALUE_DIM), lambda b, h, i: (b * nq + i, h)),
        out_shape=jax.ShapeDtypeStruct((t, HEADS * VALUE_DIM), BF16),
        scratch_shapes=[
            pltpu.VMEM((2 * tq, VALUE_DIM), BF16),
            pltpu.VMEM((2, 2 * tq, tq), F32),
            pltpu.VMEM((2, 2 * tq, tq), BF16),
            pltpu.VMEM((2, 2 * tq, LANES), F32),
            pltpu.VMEM((2 * tq, LANES), F32),
            pltpu.VMEM((2 * tq, LANES), F32),
            pltpu.VMEM((2 * tq, VALUE_DIM), F32),
        ],
        compiler_params=pltpu.CompilerParams(dimension_semantics=("arbitrary", "arbitrary", "arbitrary"),
                                             vmem_limit_bytes=VMEM_LIMIT),
        name="diff_attention",
    )(lam_params, subln_g, q, k, v)


def _outproj_kernel(x_ref, u_ref, o_ref, w_ref, g_ref, b_ref, wr_ref, br_ref,
                    x1_ref, ri_ref, rw_ref, cnt_ref, carry_scr, *, alpha, cw):
    acc = jnp.dot(u_ref[...], w_ref[0:cw, :], preferred_element_type=F32)
    acc = acc + jnp.dot(o_ref[...], w_ref[cw:, :], preferred_element_type=F32)
    x1 = _layernorm_rows(alpha * x_ref[...] + acc, g_ref[...], b_ref[...])
    x1_ref[...] = x1

    x_hi = x1.astype(BF16)
    x_lo = (x1 - x_hi.astype(F32)).astype(BF16)
    hi = jnp.dot(x_hi, wr_ref[...], preferred_element_type=F32)
    logits = (hi[:, :LANES] + hi[:, LANES:] + jnp.dot(x_lo, wr_ref[:, :LANES], preferred_element_type=F32)
              + br_ref[...])
    lane = lax.broadcasted_iota(jnp.int32, logits.shape, 1)
    neg = -jnp.inf
    gmask = lane < N_GROUPS
    lg = jnp.where(gmask, logits, neg)
    gmax = jnp.max(lg, axis=-1, keepdims=True)
    lane_f = lane.astype(F32)
    first = lambda hit: jnp.min(jnp.where(hit, lane_f, float(LANES)), axis=-1, keepdims=True).astype(jnp.int32)
    grp = first(lg == gmax)
    gsum = jnp.sum(jnp.where(gmask, jnp.exp(lg - gmax), 0.0), axis=-1, keepdims=True)
    gate = 1.0 / gsum
    emask = (lane >= N_GROUPS) & (lane < N_GROUPS + N_EXPERTS) & (((lane - N_GROUPS) // EXPERTS_PER_GROUP) == grp)
    ev = jnp.where(emask, logits, neg)
    v0 = jnp.max(ev, axis=-1, keepdims=True)
    i0 = first(emask & (ev == v0))
    emask1 = emask & (lane != i0)
    ev1 = jnp.where(emask1, logits, neg)
    v1 = jnp.max(ev1, axis=-1, keepdims=True)
    i1 = first(emask1 & (ev1 == v1))
    e1 = jnp.exp(v1 - v0)
    w0 = gate * (1.0 / (1.0 + e1))
    w1 = gate * (e1 / (1.0 + e1))

    @pl.when(pl.program_id(0) == 0)
    def _():
        carry_scr[...] = jnp.zeros(carry_scr.shape, F32)

    tm = logits.shape[0]
    oh0 = (lane == i0 - N_GROUPS).astype(F32)
    oh1 = (lane == i1 - N_GROUPS).astype(F32)
    both = oh0 + oh1
    tri = (lax.broadcasted_iota(jnp.int32, (tm, tm), 0) > lax.broadcasted_iota(jnp.int32, (tm, tm), 1)).astype(BF16)
    before = jnp.dot(tri, both.astype(BF16), preferred_element_type=F32) + carry_scr[...]
    rank0 = jnp.sum(oh0 * before, axis=-1, keepdims=True)
    rank1 = jnp.sum(oh1 * before, axis=-1, keepdims=True)
    carry_scr[...] = carry_scr[...] + jnp.sum(both, axis=0, keepdims=True)
    cnt_ref[...] = carry_scr[...]

    choice = jnp.where(lane == 0, (i0 - N_GROUPS).astype(F32), jnp.where(lane == 1, (i1 - N_GROUPS).astype(F32),
                       jnp.where(lane == 2, rank0, jnp.where(lane == 3, rank1, 0.0))))
    ri_ref[...] = jnp.transpose(choice)[0:SUBLANES, :]
    rw_ref[...] = jnp.where(lane == 0, w0, jnp.where(lane == 1, w1, 0.0))


def _outproj_router(x2, u2, o, w_bf, g, b, wr, br, *, tm, alpha):
    t, d = x2.shape
    cw = u2.shape[1]
    row = lambda i: (i, 0)
    const = lambda i: (0, 0)
    return pl.pallas_call(
        functools.partial(_outproj_kernel, alpha=alpha, cw=cw),
        grid=(t // tm,),
        in_specs=[
            pl.BlockSpec((tm, d), row),
            pl.BlockSpec((tm, cw), row),
            pl.BlockSpec((tm, o.shape[1]), row),
            pl.BlockSpec(w_bf.shape, const),
            pl.BlockSpec((1, d), const),
            pl.BlockSpec((1, d), const),
            pl.BlockSpec((d, 2 * LANES), const),
            pl.BlockSpec((1, LANES), const),
        ],
        out_specs=[
            pl.BlockSpec((tm, d), row),
            pl.BlockSpec((SUBLANES, tm), lambda i: (0, i)),
            pl.BlockSpec((tm, LANES), row),
            pl.BlockSpec((1, LANES), const),
        ],
        out_shape=[
            jax.ShapeDtypeStruct((t, d), F32),
            jax.ShapeDtypeStruct((SUBLANES, t), F32),
            jax.ShapeDtypeStruct((t, LANES), F32),
            jax.ShapeDtypeStruct((1, LANES), F32),
        ],
        scratch_shapes=[pltpu.VMEM((1, LANES), F32)],
        compiler_params=pltpu.CompilerParams(dimension_semantics=("arbitrary",), vmem_limit_bytes=VMEM_LIMIT),
        name="outproj_router",
    )(x2, u2, o, w_bf, g, b, wr, br)


def _dispatch_kernel(pos_ref, pend_ref, x_hbm, xs_hbm, xbuf, zbuf, lsem, sem, zsem, *, tm):
    i = pl.program_id(0)
    n = pl.num_programs(0)

    def load(tile, s):
        return pltpu.make_async_copy(x_hbm.at[pl.ds(pl.multiple_of(tile * tm, tm), tm)], xbuf.at[s], lsem.at[s])

    def zero_copy(e):
        start = pl.multiple_of(pend_ref[e + 1] - MOE_BLOCK, MOE_BLOCK)
        return pltpu.make_async_copy(zbuf, xs_hbm.at[pl.ds(start, MOE_BLOCK)], zsem)

    def zero_block(b):
        return pltpu.make_async_copy(zbuf, xs_hbm.at[pl.ds(pl.multiple_of(b * MOE_BLOCK, MOE_BLOCK), MOE_BLOCK)], zsem)

    @pl.when(i == 0)
    def _():
        load(0, 0).start()
        zbuf[...] = jnp.zeros(zbuf.shape, zbuf.dtype)
        first_unused = pend_ref[N_EXPERTS] // MOE_BLOCK
        n_blocks = xs_hbm.shape[0] // MOE_BLOCK
        for e in range(N_EXPERTS):
            @pl.when(pend_ref[e + 1] > pend_ref[e])
            def _():
                zero_copy(e).start()

        def start_unused(b, c):
            zero_block(b).start()
            return c

        def wait_unused(b, c):
            zero_block(b).wait()
            return c

        lax.fori_loop(first_unused, n_blocks, start_unused, 0)
        for e in range(N_EXPERTS):
            @pl.when(pend_ref[e + 1] > pend_ref[e])
            def _():
                zero_copy(e).wait()
        lax.fori_loop(first_unused, n_blocks, wait_unused, 0)

    def row_copy(s, r, dst_row):
        return pltpu.make_async_copy(xbuf.at[s, pl.ds(r % tm, 1)], xs_hbm.at[pl.ds(dst_row, 1)], sem.at[s])

    def drain(s):
        for r in range(2 * tm):
            row_copy(s, r, 0).wait()

    for slot in (0, 1):
        @pl.when(i % 2 == slot)
        def _():
            load(i, slot).wait()

            @pl.when(i > 0)
            def _():
                drain(1 - slot)

            @pl.when(i + 1 < n)
            def _():
                load(i + 1, 1 - slot).start()

            for r in range(2 * tm):
                row_copy(slot, r, pos_ref[i * 2 * tm + r]).start(priority=r % 2)

            @pl.when(i == n - 1)
            def _():
                drain(slot)


def _dispatch(pos, pad_ends, x1, *, n_rows, tm):
    t, d = x1.shape
    grid_spec = pltpu.PrefetchScalarGridSpec(
        num_scalar_prefetch=2,
        grid=(t // tm,),
        in_specs=[pl.BlockSpec(memory_space=pl.ANY)],
        out_specs=pl.BlockSpec(memory_space=pl.ANY),
        scratch_shapes=[pltpu.VMEM((2, tm, d), x1.dtype), pltpu.VMEM((MOE_BLOCK, d), x1.dtype),
                        pltpu.SemaphoreType.DMA((2,)), pltpu.SemaphoreType.DMA((2,)), pltpu.SemaphoreType.DMA(())],
    )
    return pl.pallas_call(
        functools.partial(_dispatch_kernel, tm=tm),
        grid_spec=grid_spec,
        out_shape=jax.ShapeDtypeStruct((n_rows, d), x1.dtype),
        compiler_params=pltpu.CompilerParams(dimension_semantics=("arbitrary",)),
        name="moe_dispatch",
    )(pos, pad_ends, x1)


def _expert_kernel(blk_e_ref, nused_ref, xs_ref, wg_hbm, wu_hbm, wd_hbm, y_ref,
                   wg_f32, wu_f32, wd_f32, wg_bf, wu_bf, wd_bf, sem, slot_scr, *, layer):
    i = pl.program_id(0)
    n_used = nused_ref[0]
    last_block = pl.num_programs(0) - 1

    def fetch(e, s):
        return [pltpu.make_async_copy(src.at[layer, e], dst.at[s], sem.at[s])
                for src, dst in ((wg_hbm, wg_f32), (wu_hbm, wu_f32), (wd_hbm, wd_f32))]

    @pl.when(i < n_used)
    def _():
        @pl.when((i == 0) | (blk_e_ref[i] != blk_e_ref[jnp.maximum(i - 1, 0)]))
        def _():
            e = blk_e_ref[i]

            @pl.when(i == 0)
            def _():
                slot_scr[0] = 0
                for c in fetch(e, 0):
                    c.start()

            s = slot_scr[0]
            for c in fetch(e, s):
                c.wait()

            nxt = lax.while_loop(lambda j: (j < n_used) & (blk_e_ref[jnp.minimum(j, last_block)] == e),
                                 lambda j: j + 1, i + 1)

            @pl.when(nxt < n_used)
            def _():
                for c in fetch(blk_e_ref[jnp.minimum(nxt, last_block)], 1 - s):
                    c.start()

            slot_scr[0] = 1 - s
            wg_bf[...] = wg_f32[s].astype(BF16)
            wu_bf[...] = wu_f32[s].astype(BF16)
            wd_bf[...] = wd_f32[s].astype(BF16)

        xb = xs_ref[...].astype(BF16)
        gate = jnp.dot(xb, wg_bf[...], preferred_element_type=F32)
        up = jnp.dot(xb, wu_bf[...], preferred_element_type=F32)
        h = (gate * jax.nn.sigmoid(gate)) * up
        y_ref[...] = jnp.dot(h.astype(BF16), wd_bf[...], preferred_element_type=F32)

    @pl.when(i >= nused_ref[0])
    def _():
        y_ref[...] = jnp.zeros(y_ref.shape, y_ref.dtype)


def _expert_mlp(blk_e, n_used, xs, wg, wu, wd, *, layer):
    n_rows, d = xs.shape
    n_blocks = n_rows // MOE_BLOCK
    de = wg.shape[3]
    live = lambda i, be, nu: jnp.maximum(jnp.minimum(i, nu[0] - 1), 0)
    hbm = pl.BlockSpec(memory_space=pl.ANY)
    grid_spec = pltpu.PrefetchScalarGridSpec(
        num_scalar_prefetch=2,
        grid=(n_blocks,),
        in_specs=[pl.BlockSpec((MOE_BLOCK, d), lambda i, be, nu: (live(i, be, nu), 0)), hbm, hbm, hbm],
        out_specs=pl.BlockSpec((MOE_BLOCK, d), lambda i, be, nu: (i, 0)),
        scratch_shapes=[
            pltpu.VMEM((2, d, de), F32), pltpu.VMEM((2, d, de), F32), pltpu.VMEM((2, de, d), F32),
            pltpu.VMEM((d, de), BF16), pltpu.VMEM((d, de), BF16), pltpu.VMEM((de, d), BF16),
            pltpu.SemaphoreType.DMA((2,)),
            pltpu.SMEM((1,), jnp.int32),
        ],
    )
    return pl.pallas_call(
        functools.partial(_expert_kernel, layer=layer),
        grid_spec=grid_spec,
        out_shape=jax.ShapeDtypeStruct((n_rows, d), F32),
        compiler_params=pltpu.CompilerParams(dimension_semantics=("arbitrary",), vmem_limit_bytes=VMEM_LIMIT),
        name="expert_mlp",
    )(blk_e, n_used, xs, wg, wu, wd)


def _row_gather_copy(src_hbm, dst_buf, sem, slot, row, src_row):
    return pltpu.make_async_copy(src_hbm.at[pl.ds(src_row, 1)], dst_buf.at[slot, pl.ds(row, 1)], sem.at[slot])


def _combine_kernel(pos_ref, y_hbm, x_ref, rw_ref, g_ref, b_ref, o_ref, ybuf, sem, *, tm, alpha):
    i = pl.program_id(0)
    n = pl.num_programs(0)

    def issue(tile, s):
        for r in range(2 * tm):
            _row_gather_copy(y_hbm, ybuf, sem, s, r, pos_ref[tile * 2 * tm + r]).start(priority=r % 2)

    def drain(s):
        for r in range(2 * tm):
            _row_gather_copy(y_hbm, ybuf, sem, s, r, 0).wait()

    @pl.when(i == 0)
    def _():
        issue(0, 0)
        issue(jnp.minimum(1, n - 1), 1)

    for slot in (0, 1):
        @pl.when(i % 2 == slot)
        def _():
            drain(slot)
            yy = ybuf[slot]
            rw = rw_ref[...]
            f = yy[:tm] * rw[:, 0:1] + yy[tm:] * rw[:, 1:2]
            o_ref[...] = _layernorm_rows(alpha * x_ref[...] + f, g_ref[...], b_ref[...])
            issue(jnp.minimum(i + 2, n - 1), slot)

            @pl.when(i == n - 1)
            def _():
                drain(slot)
                drain(1 - slot)


def _combine(pos, y, x1, route_w, g, b, *, tm, alpha):
    t, d = x1.shape
    grid_spec = pltpu.PrefetchScalarGridSpec(
        num_scalar_prefetch=1,
        grid=(t // tm,),
        in_specs=[
            pl.BlockSpec(memory_space=pl.ANY),
            pl.BlockSpec((tm, d), lambda i, pos: (i, 0)),
            pl.BlockSpec((tm, LANES), lambda i, pos: (i, 0)),
            pl.BlockSpec((1, d), lambda i, pos: (0, 0)),
            pl.BlockSpec((1, d), lambda i, pos: (0, 0)),
        ],
        out_specs=pl.BlockSpec((tm, d), lambda i, pos: (i, 0)),
        scratch_shapes=[pltpu.VMEM((2, 2 * tm, d), F32), pltpu.SemaphoreType.DMA((2,))],
    )
    return pl.pallas_call(
        functools.partial(_combine_kernel, tm=tm, alpha=alpha),
        grid_spec=grid_spec,
        out_shape=jax.ShapeDtypeStruct((t, d), F32),
        compiler_params=pltpu.CompilerParams(dimension_semantics=("arbitrary",), vmem_limit_bytes=VMEM_LIMIT),
        name="moe_combine",
    )(pos, y, x1, route_w, g, b)


def _tile_major(dest, tm):
    return dest.reshape(2, dest.shape[1] // tm, tm).transpose(1, 0, 2).reshape(-1)


def _dispatch_plan(route_i, counts):
    t = route_i.shape[1]
    counts = counts[0, :N_EXPERTS].astype(jnp.int32)
    padded = ((counts + MOE_BLOCK - 1) // MOE_BLOCK) * MOE_BLOCK
    pad_end = jnp.cumsum(padded)
    pad_start = pad_end - padded
    choice = route_i[0:4].astype(jnp.int32)
    dest = choice[2:4]
    for e in range(N_EXPERTS):
        dest = dest + jnp.where(choice[0:2] == e, pad_start[e], 0)
    n_blocks = -(-(2 * t) // MOE_BLOCK) + N_EXPERTS
    block_start = jnp.arange(n_blocks, dtype=jnp.int32) * MOE_BLOCK
    blk_e = jnp.minimum(jnp.sum((pad_end[None, :] <= block_start[:, None]).astype(jnp.int32), axis=1), N_EXPERTS - 1)
    pad_ends = jnp.concatenate([jnp.zeros((1,), jnp.int32), pad_end.astype(jnp.int32)])
    n_used = (pad_end[-1:] // MOE_BLOCK).astype(jnp.int32)
    return dest.astype(jnp.int32), pad_ends, blk_e, n_used, n_blocks * MOE_BLOCK


def _rope_tables(seq):
    half = HEAD_DIM // 2
    inv_freq = 1.0 / (ROPE_THETA ** (jnp.arange(half, dtype=F32) * 2.0 / HEAD_DIM))
    ang = jnp.arange(seq, dtype=F32)[:, None] * inv_freq[None, :]
    cos, sin = jnp.cos(ang), jnp.sin(ang)
    reps = LANES // HEAD_DIM
    return jnp.tile(jnp.concatenate([cos, cos], axis=1), (1, reps)), jnp.tile(jnp.concatenate([-sin, sin], axis=1), (1, reps))


def _pick_tile(n, want):
    while n % want:
        want //= 2
    return want


def kernel(x, w_in, b_in, conv_w, conv_b, conv_ln_g, conv_ln_b, lam_q1, lam_k1, lam_q2, lam_k2, subln_g, w_out, ln1_g, ln1_b, w_rg, b_rg, w_re, b_re, w_gate_e, w_up_e, w_down_e, ln2_g, ln2_b):
    batch, seq, d = x.shape
    depth = w_in.shape[0]
    t = batch * seq
    alpha = (2.0 * depth) ** 0.25
    cos_t, sin_t = _rope_tables(seq)
    tm = _pick_tile(seq, 512)
    tq = _pick_tile(seq, 512)
    tc = _pick_tile(seq, 256)
    td = _pick_tile(seq, 1024)
    row = lambda a: a.reshape(1, -1)

    x2 = x.reshape(t, d)
    for l in range(depth):
        lam_init = 0.8 - 0.6 * math.exp(-0.3 * l)
        u2, q, k, v = _inproj(x2, w_in[l].astype(BF16), row(b_in[l]), cos_t, sin_t,
                              conv_w[l].reshape(CONV_KERNEL, -1), row(conv_b[l]), row(conv_ln_g[l]),
                              row(conv_ln_b[l]), seq=seq, tm=tm)
        lam_params = jnp.stack([lam_q1[l], lam_k1[l], lam_q2[l], lam_k2[l]])
        o = _diff_attention(q, k, v, lam_params, row(subln_g[l]), batch=batch, seq=seq, tq=tq, lam_init=lam_init)
        w_router = jnp.concatenate([w_rg[l], w_re[l]], axis=1)
        w_router = jnp.pad(w_router, ((0, 0), (0, LANES - w_router.shape[1])))
        w_router_hi = w_router.astype(BF16)
        w_router = jnp.concatenate([w_router_hi, (w_router - w_router_hi.astype(F32)).astype(BF16)], axis=1)
        b_router = jnp.pad(jnp.concatenate([b_rg[l], b_re[l]]), (0, LANES - N_GROUPS - N_EXPERTS))
        x1, route_i, route_w, counts = _outproj_router(x2, u2, o, w_out[l].astype(BF16), row(ln1_g[l]), row(ln1_b[l]),
                                                       w_router, row(b_router), tm=tm, alpha=alpha)
        dest, pad_ends, blk_e, n_used, n_rows = _dispatch_plan(route_i, counts)
        xs = _dispatch(_tile_major(dest, td), pad_ends, x1, n_rows=n_rows, tm=td)
        y = _expert_mlp(blk_e, n_used, xs, w_gate_e, w_up_e, w_down_e, layer=l)
        x2 = _combine(_tile_major(dest, tc), y, x1, route_w, row(ln2_g[l]), row(ln2_b[l]), tm=tc, alpha=alpha)
    return x2.reshape(batch, seq, d)
```

```python
import functools
import math

import jax
import jax.numpy as jnp
from jax import lax
from jax.experimental import pallas as pl
from jax.experimental.pallas import tpu as pltpu

F32 = jnp.float32
BF16 = jnp.bfloat16

CHUNK = 64
CONV_KERNEL = 31
HEADS = 4
HEAD_DIM = 64
VALUE_DIM = 2 * HEAD_DIM
ROPE_THETA = 10000.0
N_GROUPS = 4
EXPERTS_PER_GROUP = 8
N_EXPERTS = N_GROUPS * EXPERTS_PER_GROUP
MOE_BLOCK = 256
LN_EPS = 1e-5
LANES = 128
SUBLANES = 8
CONV_HALO = 32
VMEM_LIMIT = 56 * 1024 * 1024


def _pack_bf16_pairs(x):
    c = x.shape[1] // 2
    bits = lambda v: lax.bitcast_convert_type(v.astype(BF16).astype(F32), jnp.uint32)
    return (bits(x[:, c:]) & jnp.uint32(0xFFFF0000)) | (bits(x[:, :c]) >> 16)


def _unpack_bf16_pairs(w):
    lo = lax.bitcast_convert_type(w << 16, F32)
    hi = lax.bitcast_convert_type(w & jnp.uint32(0xFFFF0000), F32)
    return jnp.concatenate([lo, hi], axis=1).astype(BF16)


def _layernorm_rows(y, g, b):
    mu = jnp.mean(y, axis=-1, keepdims=True)
    d = y - mu
    var = jnp.mean(d * d, axis=-1, keepdims=True)
    return d * lax.rsqrt(var + LN_EPS) * g + b


def _inproj_kernel(x_ref, w_ref, b_ref, cos_ref, sin_ref, cw_ref, cb_ref, cg_ref, cbeta_ref,
                   u_ref, q_ref, k_ref, v_ref, buf_ref, sh_ref, wtab_ref, *, cw, qk, tiles_per_seq, rb):
    i = pl.program_id(0)
    tm = x_ref.shape[0]
    xb = x_ref[...].astype(BF16)

    def proj(c0, n):
        return jnp.dot(xb, w_ref[:, c0:c0 + n], preferred_element_type=F32) + b_ref[:, c0:c0 + n]

    @pl.when(i % tiles_per_seq == 0)
    def _():
        buf_ref[0:CONV_HALO, :] = jnp.zeros((CONV_HALO, buf_ref.shape[1]), F32)

    @pl.when(i % tiles_per_seq != 0)
    def _():
        buf_ref[0:CONV_HALO, :] = buf_ref[tm:tm + CONV_HALO, :]

    ag = proj(0, 2 * cw)
    buf_ref[CONV_HALO:CONV_HALO + tm, :] = ag[:, :cw] * jax.nn.sigmoid(ag[:, cw:])

    first = CONV_HALO - (CONV_KERNEL - 1)
    span = tm + CONV_HALO - SUBLANES
    for p in range(1, SUBLANES):
        sh_ref[p - 1, 0:span, :] = buf_ref[p:p + span, :]

    @pl.when(i == 0)
    def _():
        for j in range(CONV_KERNEL):
            wtab_ref[j] = jnp.broadcast_to(cw_ref[j:j + 1, :], wtab_ref.shape[1:])

    bias = cb_ref[...]
    g = cg_ref[...]
    beta = cbeta_ref[...]
    for r0 in range(0, tm, rb):
        acc = jnp.broadcast_to(bias, (rb, bias.shape[1]))
        for j in range(CONV_KERNEL):
            p = (first + j) % SUBLANES
            a0 = r0 + first + j - p
            window = buf_ref[a0:a0 + rb, :] if p == 0 else sh_ref[p - 1, a0:a0 + rb, :]
            acc = acc + jnp.concatenate([wtab_ref[j]] * (rb // SUBLANES), axis=0) * window
        y = _layernorm_rows(acc, g, beta)
        u_ref[r0:r0 + rb, :] = (y * jax.nn.sigmoid(y)).astype(u_ref.dtype)

    cos = cos_ref[...]
    sin = sin_ref[...]
    lane = lax.broadcasted_iota(jnp.int32, cos.shape, 1)
    first_half = (lane % HEAD_DIM) < (HEAD_DIM // 2)

    def rope(z, scale):
        outs = []
        for c in range(0, qk, LANES):
            zz = z[:, c:c + LANES]
            swapped = jnp.where(first_half, pltpu.roll(zz, LANES - HEAD_DIM // 2, 1),
                                pltpu.roll(zz, HEAD_DIM // 2, 1))
            outs.append((zz * cos + swapped * sin) * scale)
        return jnp.concatenate(outs, axis=1)

    q_ref[...] = rope(proj(2 * cw, qk), HEAD_DIM ** -0.5 * math.log2(math.e)).astype(BF16)
    k_ref[...] = rope(proj(2 * cw + qk, qk), 1.0).astype(BF16)
    v_ref[...] = proj(2 * cw + 2 * qk, v_ref.shape[1]).astype(BF16)


def _inproj(x2, w_bf, b, cos_t, sin_t, conv_w, conv_b, conv_g, conv_beta, *, seq, tm, rb=32):
    t, d = x2.shape
    n = w_bf.shape[1]
    cw = d // 2
    qk = 2 * HEADS * HEAD_DIM
    vw = HEADS * VALUE_DIM
    tiles_per_seq = seq // tm
    const = lambda i: (0, 0)
    return pl.pallas_call(
        functools.partial(_inproj_kernel, cw=cw, qk=qk, tiles_per_seq=tiles_per_seq, rb=rb),
        grid=(t // tm,),
        in_specs=[
            pl.BlockSpec((tm, d), lambda i: (i, 0)),
            pl.BlockSpec((d, n), const),
            pl.BlockSpec((1, n), const),
            pl.BlockSpec((tm, LANES), lambda i: (i % tiles_per_seq, 0)),
            pl.BlockSpec((tm, LANES), lambda i: (i % tiles_per_seq, 0)),
            pl.BlockSpec((CONV_KERNEL, cw), const),
            pl.BlockSpec((1, cw), const),
            pl.BlockSpec((1, cw), const),
            pl.BlockSpec((1, cw), const),
        ],
        out_specs=[
            pl.BlockSpec((tm, cw), lambda i: (i, 0)),
            pl.BlockSpec((tm, qk), lambda i: (i, 0)),
            pl.BlockSpec((tm, qk), lambda i: (i, 0)),
            pl.BlockSpec((tm, vw), lambda i: (i, 0)),
        ],
        out_shape=[
            jax.ShapeDtypeStruct((t, cw), BF16),
            jax.ShapeDtypeStruct((t, qk), BF16),
            jax.ShapeDtypeStruct((t, qk), BF16),
            jax.ShapeDtypeStruct((t, vw), BF16),
        ],
        scratch_shapes=[pltpu.VMEM((CONV_HALO + tm, cw), F32), pltpu.VMEM((SUBLANES - 1, CONV_HALO + tm, cw), F32),
                        pltpu.VMEM((CONV_KERNEL, SUBLANES, cw), F32)],
        compiler_params=pltpu.CompilerParams(dimension_semantics=("arbitrary",), vmem_limit_bytes=VMEM_LIMIT),
        name="inproj_conv",
    )(x2, w_bf, b, cos_t, sin_t, conv_w, conv_b, conv_g, conv_beta)


def _attn_kernel(lam_ref, g_ref, q_ref, k_ref, v_ref, o_ref, qq_scr, s_scr, p_scr, a_scr, m_scr, l_scr, acc_scr,
                 *, tq, rc, lam_init):
    qi = pl.program_id(2)
    q = q_ref[...]
    lane = lax.broadcasted_iota(jnp.int32, q.shape, 1)
    zero = jnp.zeros_like(q)
    qq_scr[0:tq, :] = jnp.where(lane < HEAD_DIM, q, zero)
    qq_scr[tq:, :] = jnp.where(lane >= HEAD_DIM, q, zero)

    m_scr[...] = jnp.full(m_scr.shape, -jnp.inf, F32)
    l_scr[...] = jnp.zeros(l_scr.shape, F32)
    acc_scr[...] = jnp.zeros(acc_scr.shape, F32)

    def kv_rows(j):
        return pl.ds(pl.multiple_of(j * tq, tq), tq)

    def scores(j, slot):
        s_scr[slot] = lax.dot_general(qq_scr[...], k_ref[kv_rows(j), :], (((1,), (1,)), ((), ())),
                                      preferred_element_type=F32)

    def softmax(slot, masked):
        for r0 in range(0, 2 * tq, rc):
            s = s_scr[slot, r0:r0 + rc, :]
            if masked:
                row = (lax.broadcasted_iota(jnp.int32, s.shape, 0) + r0) % tq
                col = lax.broadcasted_iota(jnp.int32, s.shape, 1)
                s = jnp.where((col // CHUNK) <= (row // CHUNK), s, -jnp.inf)
            m_prev = m_scr[r0:r0 + rc, :]
            m_new = jnp.maximum(m_prev, jnp.max(s, axis=-1, keepdims=True))
            alpha = jnp.exp2(m_prev - m_new)
            p = jnp.exp2(s - jnp.tile(m_new, (1, tq // LANES)))
            l_scr[r0:r0 + rc, :] = alpha * l_scr[r0:r0 + rc, :] + jnp.sum(p, axis=-1, keepdims=True)
            m_scr[r0:r0 + rc, :] = m_new
            a_scr[slot, r0:r0 + rc, :] = alpha
            p_scr[slot, r0:r0 + rc, :] = p.astype(BF16)

    def values(j, slot):
        acc_scr[...] = a_scr[slot] * acc_scr[...] + jnp.dot(p_scr[slot], v_ref[kv_rows(j), :],
                                                            preferred_element_type=F32)

    def step(i, par):
        softmax(par, False)
        values(i - 1, 1 - par)
        scores(i + 1, 1 - par)

    def finish(par):
        softmax(par, True)
        values(qi - 1, 1 - par)
        values(qi, par)

    @pl.when(qi == 0)
    def _():
        scores(0, 0)
        softmax(0, True)
        values(0, 0)

    @pl.when(qi > 0)
    def _():
        scores(0, 0)
        scores(1, 1)
        softmax(0, False)

    def body(i, carry):
        for par in (0, 1):
            @pl.when(i % 2 == par)
            def _():
                step(i, par)
        return carry

    lax.fori_loop(1, qi, body, 0)

    for par in (0, 1):
        @pl.when((qi > 0) & (qi % 2 == par))
        def _():
            finish(par)

    lp = lam_ref[...]
    lam = (jnp.exp(jnp.sum(lp[0:1] * lp[1:2], axis=-1, keepdims=True))
           - jnp.exp(jnp.sum(lp[2:3] * lp[3:4], axis=-1, keepdims=True)) + lam_init)
    o = acc_scr[...] / l_scr[...]
    o = o[:tq] - lam * o[tq:]
    ms = jnp.mean(o * o, axis=-1, keepdims=True)
    o_ref[...] = (o * lax.rsqrt(ms + LN_EPS) * g_ref[...] * (1.0 - lam_init)).astype(o_ref.dtype)


def _diff_attention(q, k, v, lam_params, subln_g, *, batch, seq, tq, lam_init):
    t = q.shape[0]
    nq = seq // tq
    return pl.pallas_call(
        functools.partial(_attn_kernel, tq=tq, rc=min(256, tq), lam_init=lam_init),
        grid=(batch, HEADS, nq),
        in_specs=[
            pl.BlockSpec((4, HEAD_DIM), lambda b, h, i: (0, 0)),
            pl.BlockSpec((1, VALUE_DIM), lambda b, h, i: (0, 0)),
            pl.BlockSpec((tq, VALUE_DIM), lambda b, h, i: (b * nq + i, h)),
            pl.BlockSpec((seq, VALUE_DIM), lambda b, h, i: (b, h)),
            pl.BlockSpec((seq, VALUE_DIM), lambda b, h, i: (b, h)),
        ],
        out_specs=pl.BlockSpec((tq, VALUE_DIM), lambda b, h, i: (b * nq + i, h)),
        out_shape=jax.ShapeDtypeStruct((t, HEADS * VALUE_DIM), BF16),
        scratch_shapes=[
            pltpu.VMEM((2 * tq, VALUE_DIM), BF16),
            pltpu.VMEM((2, 2 * tq, tq), F32),
            pltpu.VMEM((2, 2 * tq, tq), BF16),
            pltpu.VMEM((2, 2 * tq, LANES), F32),
            pltpu.VMEM((2 * tq, LANES), F32),
            pltpu.VMEM((2 * tq, LANES), F32),
            pltpu.VMEM((2 * tq, VALUE_DIM), F32),
        ],
        compiler_params=pltpu.CompilerParams(dimension_semantics=("arbitrary", "arbitrary", "arbitrary"),
                                             vmem_limit_bytes=VMEM_LIMIT),
        name="diff_attention",
    )(lam_params, subln_g, q, k, v)


def _outproj_kernel(x_ref, u_ref, o_ref, w_ref, g_ref, b_ref, wr_ref, br_ref,
                    x1_ref, ri_ref, rw_ref, cnt_ref, carry_scr, *, alpha, cw):
    acc = jnp.dot(u_ref[...], w_ref[0:cw, :], preferred_element_type=F32)
    acc = acc + jnp.dot(o_ref[...], w_ref[cw:, :], preferred_element_type=F32)
    x1 = _layernorm_rows(alpha * x_ref[...] + acc, g_ref[...], b_ref[...])
    x1_ref[...] = x1

    x_hi = x1.astype(BF16)
    x_lo = (x1 - x_hi.astype(F32)).astype(BF16)
    hi = jnp.dot(x_hi, wr_ref[...], preferred_element_type=F32)
    logits = (hi[:, :LANES] + hi[:, LANES:] + jnp.dot(x_lo, wr_ref[:, :LANES], preferred_element_type=F32)
              + br_ref[...])
    lane = lax.broadcasted_iota(jnp.int32, logits.shape, 1)
    neg = -jnp.inf
    gmask = lane < N_GROUPS
    lg = jnp.where(gmask, logits, neg)
    gmax = jnp.max(lg, axis=-1, keepdims=True)
    lane_f = lane.astype(F32)
    first = lambda hit: jnp.min(jnp.where(hit, lane_f, float(LANES)), axis=-1, keepdims=True).astype(jnp.int32)
    grp = first(lg == gmax)
    gsum = jnp.sum(jnp.where(gmask, jnp.exp(lg - gmax), 0.0), axis=-1, keepdims=True)
    gate = 1.0 / gsum
    emask = (lane >= N_GROUPS) & (lane < N_GROUPS + N_EXPERTS) & (((lane - N_GROUPS) // EXPERTS_PER_GROUP) == grp)
    ev = jnp.where(emask, logits, neg)
    v0 = jnp.max(ev, axis=-1, keepdims=True)
    i0 = first(emask & (ev == v0))
    emask1 = emask & (lane != i0)
    ev1 = jnp.where(emask1, logits, neg)
    v1 = jnp.max(ev1, axis=-1, keepdims=True)
    i1 = first(emask1 & (ev1 == v1))
    e1 = jnp.exp(v1 - v0)
    w0 = gate * (1.0 / (1.0 + e1))
    w1 = gate * (e1 / (1.0 + e1))

    @pl.when(pl.program_id(0) == 0)
    def _():
        carry_scr[...] = jnp.zeros(carry_scr.shape, F32)

    tm = logits.shape[0]
    oh0 = (lane == i0 - N_GROUPS).astype(F32)
    oh1 = (lane == i1 - N_GROUPS).astype(F32)
    both = oh0 + oh1
    tri = (lax.broadcasted_iota(jnp.int32, (tm, tm), 0) > lax.broadcasted_iota(jnp.int32, (tm, tm), 1)).astype(BF16)
    before = jnp.dot(tri, both.astype(BF16), preferred_element_type=F32) + carry_scr[...]
    rank0 = jnp.sum(oh0 * before, axis=-1, keepdims=True)
    rank1 = jnp.sum(oh1 * before, axis=-1, keepdims=True)
    carry_scr[...] = carry_scr[...] + jnp.sum(both, axis=0, keepdims=True)
    cnt_ref[...] = carry_scr[...]

    choice = jnp.where(lane == 0, (i0 - N_GROUPS).astype(F32), jnp.where(lane == 1, (i1 - N_GROUPS).astype(F32),
                       jnp.where(lane == 2, rank0, jnp.where(lane == 3, rank1, 0.0))))
    ri_ref[...] = jnp.transpose(choice)[0:SUBLANES, :]
    rw_ref[...] = jnp.where(lane == 0, w0, jnp.where(lane == 1, w1, 0.0))


def _outproj_router(x2, u2, o, w_bf, g, b, wr, br, *, tm, alpha):
    t, d = x2.shape
    cw = u2.shape[1]
    row = lambda i: (i, 0)
    const = lambda i: (0, 0)
    return pl.pallas_call(
        functools.partial(_outproj_kernel, alpha=alpha, cw=cw),
        grid=(t // tm,),
        in_specs=[
            pl.BlockSpec((tm, d), row),
            pl.BlockSpec((tm, cw), row),
            pl.BlockSpec((tm, o.shape[1]), row),
            pl.BlockSpec(w_bf.shape, const),
            pl.BlockSpec((1, d), const),
            pl.BlockSpec((1, d), const),
            pl.BlockSpec((d, 2 * LANES), const),
            pl.BlockSpec((1, LANES), const),
        ],
        out_specs=[
            pl.BlockSpec((tm, d), row),
            pl.BlockSpec((SUBLANES, tm), lambda i: (0, i)),
            pl.BlockSpec((tm, LANES), row),
            pl.BlockSpec((1, LANES), const),
        ],
        out_shape=[
            jax.ShapeDtypeStruct((t, d), F32),
            jax.ShapeDtypeStruct((SUBLANES, t), F32),
            jax.ShapeDtypeStruct((t, LANES), F32),
            jax.ShapeDtypeStruct((1, LANES), F32),
        ],
        scratch_shapes=[pltpu.VMEM((1, LANES), F32)],
        compiler_params=pltpu.CompilerParams(dimension_semantics=("arbitrary",), vmem_limit_bytes=VMEM_LIMIT),
        name="outproj_router",
    )(x2, u2, o, w_bf, g, b, wr, br)


def _dispatch_kernel(pos_ref, pend_ref, x_hbm, xs_hbm, xbuf, pbuf, zbuf, lsem, sem, zsem, *, tm, rp):
    i = pl.program_id(0)
    n = pl.num_programs(0)

    def load(tile, s):
        return pltpu.make_async_copy(x_hbm.at[pl.ds(pl.multiple_of(tile * tm, tm), tm)], xbuf.at[s], lsem.at[s])

    def zero_copy(e):
        start = pl.multiple_of(pend_ref[e + 1] - MOE_BLOCK, MOE_BLOCK)
        return pltpu.make_async_copy(zbuf, xs_hbm.at[pl.ds(start, MOE_BLOCK)], zsem)

    def zero_block(b):
        return pltpu.make_async_copy(zbuf, xs_hbm.at[pl.ds(pl.multiple_of(b * MOE_BLOCK, MOE_BLOCK), MOE_BLOCK)], zsem)

    @pl.when(i == 0)
    def _():
        load(0, 0).start()
        zbuf[...] = jnp.zeros(zbuf.shape, zbuf.dtype)
        first_unused = pend_ref[N_EXPERTS] // MOE_BLOCK
        n_blocks = xs_hbm.shape[0] // MOE_BLOCK
        for e in range(N_EXPERTS):
            @pl.when(pend_ref[e + 1] > pend_ref[e])
            def _():
                zero_copy(e).start()

        def start_unused(b, c):
            zero_block(b).start()
            return c

        def wait_unused(b, c):
            zero_block(b).wait()
            return c

        lax.fori_loop(first_unused, n_blocks, start_unused, 0)
        for e in range(N_EXPERTS):
            @pl.when(pend_ref[e + 1] > pend_ref[e])
            def _():
                zero_copy(e).wait()
        lax.fori_loop(first_unused, n_blocks, wait_unused, 0)

    def row_copy(s, r, dst_row):
        return pltpu.make_async_copy(pbuf.at[s, pl.ds(r % tm, 1)], xs_hbm.at[pl.ds(dst_row, 1)], sem.at[s])

    def drain(s):
        for r in range(2 * tm):
            row_copy(s, r, 0).wait()

    for slot in (0, 1):
        @pl.when(i % 2 == slot)
        def _():
            load(i, slot).wait()

            @pl.when(i > 0)
            def _():
                drain(1 - slot)

            @pl.when(i + 1 < n)
            def _():
                load(i + 1, 1 - slot).start()

            for r0 in range(0, tm, rp):
                pbuf[slot, r0:r0 + rp, :] = _pack_bf16_pairs(xbuf[slot, r0:r0 + rp, :])

            for r in range(2 * tm):
                row_copy(slot, r, pos_ref[i * 2 * tm + r]).start(priority=r % 2)

            @pl.when(i == n - 1)
            def _():
                drain(slot)


def _dispatch(pos, pad_ends, x1, *, n_rows, tm):
    t, d = x1.shape
    grid_spec = pltpu.PrefetchScalarGridSpec(
        num_scalar_prefetch=2,
        grid=(t // tm,),
        in_specs=[pl.BlockSpec(memory_space=pl.ANY)],
        out_specs=pl.BlockSpec(memory_space=pl.ANY),
        scratch_shapes=[pltpu.VMEM((2, tm, d), x1.dtype), pltpu.VMEM((2, tm, d // 2), jnp.uint32),
                        pltpu.VMEM((MOE_BLOCK, d // 2), jnp.uint32),
                        pltpu.SemaphoreType.DMA((2,)), pltpu.SemaphoreType.DMA((2,)), pltpu.SemaphoreType.DMA(())],
    )
    return pl.pallas_call(
        functools.partial(_dispatch_kernel, tm=tm, rp=min(128, tm)),
        grid_spec=grid_spec,
        out_shape=jax.ShapeDtypeStruct((n_rows, d // 2), jnp.uint32),
        compiler_params=pltpu.CompilerParams(dimension_semantics=("arbitrary",)),
        name="moe_dispatch",
    )(pos, pad_ends, x1)


def _expert_kernel(blk_e_ref, nused_ref, xs_ref, wg_hbm, wu_hbm, wd_hbm, y_ref,
                   wg_f32, wu_f32, wd_f32, wg_bf, wu_bf, wd_bf, sem, slot_scr, *, layer):
    i = pl.program_id(0)
    n_used = nused_ref[0]
    last_block = pl.num_programs(0) - 1

    def fetch(e, s):
        return [pltpu.make_async_copy(src.at[layer, e], dst.at[s], sem.at[s])
                for src, dst in ((wg_hbm, wg_f32), (wu_hbm, wu_f32), (wd_hbm, wd_f32))]

    @pl.when(i < n_used)
    def _():
        @pl.when((i == 0) | (blk_e_ref[i] != blk_e_ref[jnp.maximum(i - 1, 0)]))
        def _():
            e = blk_e_ref[i]

            @pl.when(i == 0)
            def _():
                slot_scr[0] = 0
                for c in fetch(e, 0):
                    c.start()

            s = slot_scr[0]
            for c in fetch(e, s):
                c.wait()

            nxt = lax.while_loop(lambda j: (j < n_used) & (blk_e_ref[jnp.minimum(j, last_block)] == e),
                                 lambda j: j + 1, i + 1)

            @pl.when(nxt < n_used)
            def _():
                for c in fetch(blk_e_ref[jnp.minimum(nxt, last_block)], 1 - s):
                    c.start()

            slot_scr[0] = 1 - s
            wg_bf[...] = wg_f32[s].astype(BF16)
            wu_bf[...] = wu_f32[s].astype(BF16)
            wd_bf[...] = wd_f32[s].astype(BF16)

        xb = _unpack_bf16_pairs(xs_ref[...])
        gate = jnp.dot(xb, wg_bf[...], preferred_element_type=F32)
        up = jnp.dot(xb, wu_bf[...], preferred_element_type=F32)
        h = (gate * jax.nn.sigmoid(gate)) * up
        y_ref[...] = jnp.dot(h.astype(BF16), wd_bf[...], preferred_element_type=F32)

    @pl.when(i >= nused_ref[0])
    def _():
        y_ref[...] = jnp.zeros(y_ref.shape, y_ref.dtype)


def _expert_mlp(blk_e, n_used, xs, wg, wu, wd, *, layer):
    n_rows = xs.shape[0]
    d = wg.shape[2]
    n_blocks = n_rows // MOE_BLOCK
    de = wg.shape[3]
    live = lambda i, be, nu: jnp.maximum(jnp.minimum(i, nu[0] - 1), 0)
    hbm = pl.BlockSpec(memory_space=pl.ANY)
    grid_spec = pltpu.PrefetchScalarGridSpec(
        num_scalar_prefetch=2,
        grid=(n_blocks,),
        in_specs=[pl.BlockSpec((MOE_BLOCK, xs.shape[1]), lambda i, be, nu: (live(i, be, nu), 0)), hbm, hbm, hbm],
        out_specs=pl.BlockSpec((MOE_BLOCK, d), lambda i, be, nu: (i, 0)),
        scratch_shapes=[
            pltpu.VMEM((2, d, de), F32), pltpu.VMEM((2, d, de), F32), pltpu.VMEM((2, de, d), F32),
            pltpu.VMEM((d, de), BF16), pltpu.VMEM((d, de), BF16), pltpu.VMEM((de, d), BF16),
            pltpu.SemaphoreType.DMA((2,)),
            pltpu.SMEM((1,), jnp.int32),
        ],
    )
    return pl.pallas_call(
        functools.partial(_expert_kernel, layer=layer),
        grid_spec=grid_spec,
        out_shape=jax.ShapeDtypeStruct((n_rows, d), F32),
        compiler_params=pltpu.CompilerParams(dimension_semantics=("arbitrary",), vmem_limit_bytes=VMEM_LIMIT),
        name="expert_mlp",
    )(blk_e, n_used, xs, wg, wu, wd)


def _row_gather_copy(src_hbm, dst_buf, sem, slot, row, src_row):
    return pltpu.make_async_copy(src_hbm.at[pl.ds(src_row, 1)], dst_buf.at[slot, pl.ds(row, 1)], sem.at[slot])


def _combine_kernel(pos_ref, y_hbm, x_ref, rw_ref, g_ref, b_ref, o_ref, ybuf, sem, *, tm, alpha):
    i = pl.program_id(0)
    n = pl.num_programs(0)

    def issue(tile, s):
        for r in range(2 * tm):
            _row_gather_copy(y_hbm, ybuf, sem, s, r, pos_ref[tile * 2 * tm + r]).start(priority=r % 2)

    def drain(s):
        for r in range(2 * tm):
            _row_gather_copy(y_hbm, ybuf, sem, s, r, 0).wait()

    @pl.when(i == 0)
    def _():
        issue(0, 0)
        issue(jnp.minimum(1, n - 1), 1)

    for slot in (0, 1):
        @pl.when(i % 2 == slot)
        def _():
            drain(slot)
            yy = ybuf[slot]
            rw = rw_ref[...]
            f = yy[:tm] * rw[:, 0:1] + yy[tm:] * rw[:, 1:2]
            o_ref[...] = _layernorm_rows(alpha * x_ref[...] + f, g_ref[...], b_ref[...])
            issue(jnp.minimum(i + 2, n - 1), slot)

            @pl.when(i == n - 1)
            def _():
                drain(slot)
                drain(1 - slot)


def _combine(pos, y, x1, route_w, g, b, *, tm, alpha):
    t, d = x1.shape
    grid_spec = pltpu.PrefetchScalarGridSpec(
        num_scalar_prefetch=1,
        grid=(t // tm,),
        in_specs=[
            pl.BlockSpec(memory_space=pl.ANY),
            pl.BlockSpec((tm, d), lambda i, pos: (i, 0)),
            pl.BlockSpec((tm, LANES), lambda i, pos: (i, 0)),
            pl.BlockSpec((1, d), lambda i, pos: (0, 0)),
            pl.BlockSpec((1, d), lambda i, pos: (0, 0)),
        ],
        out_specs=pl.BlockSpec((tm, d), lambda i, pos: (i, 0)),
        scratch_shapes=[pltpu.VMEM((2, 2 * tm, d), F32), pltpu.SemaphoreType.DMA((2,))],
    )
    return pl.pallas_call(
        functools.partial(_combine_kernel, tm=tm, alpha=alpha),
        grid_spec=grid_spec,
        out_shape=jax.ShapeDtypeStruct((t, d), F32),
        compiler_params=pltpu.CompilerParams(dimension_semantics=("arbitrary",), vmem_limit_bytes=VMEM_LIMIT),
        name="moe_combine",
    )(pos, y, x1, route_w, g, b)


def _tile_major(dest, tm):
    return dest.reshape(2, dest.shape[1] // tm, tm).transpose(1, 0, 2).reshape(-1)


def _dispatch_plan(route_i, counts):
    t = route_i.shape[1]
    counts = counts[0, :N_EXPERTS].astype(jnp.int32)
    padded = ((counts + MOE_BLOCK - 1) // MOE_BLOCK) * MOE_BLOCK
    pad_end = jnp.cumsum(padded)
    pad_start = pad_end - padded
    choice = route_i[0:4].astype(jnp.int32)
    dest = choice[2:4]
    for e in range(N_EXPERTS):
        dest = dest + jnp.where(choice[0:2] == e, pad_start[e], 0)
    n_blocks = -(-(2 * t) // MOE_BLOCK) + N_EXPERTS
    block_start = jnp.arange(n_blocks, dtype=jnp.int32) * MOE_BLOCK
    blk_e = jnp.minimum(jnp.sum((pad_end[None, :] <= block_start[:, None]).astype(jnp.int32), axis=1), N_EXPERTS - 1)
    pad_ends = jnp.concatenate([jnp.zeros((1,), jnp.int32), pad_end.astype(jnp.int32)])
    n_used = (pad_end[-1:] // MOE_BLOCK).astype(jnp.int32)
    return dest.astype(jnp.int32), pad_ends, blk_e, n_used, n_blocks * MOE_BLOCK


def _rope_tables(seq):
    half = HEAD_DIM // 2
    inv_freq = 1.0 / (ROPE_THETA ** (jnp.arange(half, dtype=F32) * 2.0 / HEAD_DIM))
    ang = jnp.arange(seq, dtype=F32)[:, None] * inv_freq[None, :]
    cos, sin = jnp.cos(ang), jnp.sin(ang)
    reps = LANES // HEAD_DIM
    return jnp.tile(jnp.concatenate([cos, cos], axis=1), (1, reps)), jnp.tile(jnp.concatenate([-sin, sin], axis=1), (1, reps))


def _pick_tile(n, want):
    while n % want:
        want //= 2
    return want


def kernel(x, w_in, b_in, conv_w, conv_b, conv_ln_g, conv_ln_b, lam_q1, lam_k1, lam_q2, lam_k2, subln_g, w_out, ln1_g, ln1_b, w_rg, b_rg, w_re, b_re, w_gate_e, w_up_e, w_down_e, ln2_g, ln2_b):
    batch, seq, d = x.shape
    depth = w_in.shape[0]
    t = batch * seq
    alpha = (2.0 * depth) ** 0.25
    cos_t, sin_t = _rope_tables(seq)
    tm = _pick_tile(seq, 512)
    tq = _pick_tile(seq, 512)
    tc = _pick_tile(seq, 256)
    td = _pick_tile(seq, 1024)
    row = lambda a: a.reshape(1, -1)

    x2 = x.reshape(t, d)
    for l in range(depth):
        lam_init = 0.8 - 0.6 * math.exp(-0.3 * l)
        u2, q, k, v = _inproj(x2, w_in[l].astype(BF16), row(b_in[l]), cos_t, sin_t,
                              conv_w[l].reshape(CONV_KERNEL, -1), row(conv_b[l]), row(conv_ln_g[l]),
                              row(conv_ln_b[l]), seq=seq, tm=tm)
        lam_params = jnp.stack([lam_q1[l], lam_k1[l], lam_q2[l], lam_k2[l]])
        o = _diff_attention(q, k, v, lam_params, row(subln_g[l]), batch=batch, seq=seq, tq=tq, lam_init=lam_init)
        w_router = jnp.concatenate([w_rg[l], w_re[l]], axis=1)
        w_router = jnp.pad(w_router, ((0, 0), (0, LANES - w_router.shape[1])))
        w_router_hi = w_router.astype(BF16)
        w_router = jnp.concatenate([w_router_hi, (w_router - w_router_hi.astype(F32)).astype(BF16)], axis=1)
        b_router = jnp.pad(jnp.concatenate([b_rg[l], b_re[l]]), (0, LANES - N_GROUPS - N_EXPERTS))
        x1, route_i, route_w, counts = _outproj_router(x2, u2, o, w_out[l].astype(BF16), row(ln1_g[l]), row(ln1_b[l]),
                                                       w_router, row(b_router), tm=tm, alpha=alpha)
        dest, pad_ends, blk_e, n_used, n_rows = _dispatch_plan(route_i, counts)
        xs = _dispatch(_tile_major(dest, td), pad_ends, x1, n_rows=n_rows, tm=td)
        y = _expert_mlp(blk_e, n_used, xs, w_gate_e, w_up_e, w_down_e, layer=l)
        x2 = _combine(_tile_major(dest, tc), y, x1, route_w, row(ln2_g[l]), row(ln2_b[l]), tm=tc, alpha=alpha)
    return x2.reshape(batch, seq, d)
```

```python
import functools
import math

import jax
import jax.numpy as jnp
from jax import lax
from jax.experimental import pallas as pl
from jax.experimental.pallas import tpu as pltpu

F32 = jnp.float32
BF16 = jnp.bfloat16

CHUNK = 64
CONV_KERNEL = 31
HEADS = 4
HEAD_DIM = 64
VALUE_DIM = 2 * HEAD_DIM
ROPE_THETA = 10000.0
N_GROUPS = 4
EXPERTS_PER_GROUP = 8
N_EXPERTS = N_GROUPS * EXPERTS_PER_GROUP
MOE_BLOCK = 256
LN_EPS = 1e-5
LANES = 128
SUBLANES = 8
CONV_HALO = 32
VMEM_LIMIT = 56 * 1024 * 1024
PROJ_ROWS = 512
ATTN_ROWS = 512
COMBINE_ROWS = 256
DISPATCH_ROWS = 1024


def _pack_bf16_pairs(x):
    c = x.shape[1] // 2
    bits = lambda v: lax.bitcast_convert_type(v.astype(BF16).astype(F32), jnp.uint32)
    return (bits(x[:, c:]) & jnp.uint32(0xFFFF0000)) | (bits(x[:, :c]) >> 16)


def _unpack_bf16_pairs(w):
    lo = lax.bitcast_convert_type(w << 16, F32)
    hi = lax.bitcast_convert_type(w & jnp.uint32(0xFFFF0000), F32)
    return jnp.concatenate([lo, hi], axis=1).astype(BF16)


def _layernorm_rows(y, g, b):
    mu = jnp.mean(y, axis=-1, keepdims=True)
    d = y - mu
    var = jnp.mean(d * d, axis=-1, keepdims=True)
    return d * lax.rsqrt(var + LN_EPS) * g + b


def _inproj_kernel(x_ref, w_ref, b_ref, cos_ref, sin_ref, cw_ref, cb_ref, cg_ref, cbeta_ref,
                   u_ref, q_ref, k_ref, v_ref, buf_ref, sh_ref, wtab_ref, *, cw, qk, tiles_per_seq, rb):
    i = pl.program_id(0)
    tm = x_ref.shape[0]
    xb = x_ref[...].astype(BF16)

    def proj(c0, n):
        return jnp.dot(xb, w_ref[:, c0:c0 + n], preferred_element_type=F32) + b_ref[:, c0:c0 + n]

    @pl.when(i % tiles_per_seq == 0)
    def _():
        buf_ref[0:CONV_HALO, :] = jnp.zeros((CONV_HALO, buf_ref.shape[1]), F32)

    @pl.when(i % tiles_per_seq != 0)
    def _():
        buf_ref[0:CONV_HALO, :] = buf_ref[tm:tm + CONV_HALO, :]

    ag = proj(0, 2 * cw)
    buf_ref[CONV_HALO:CONV_HALO + tm, :] = ag[:, :cw] * jax.nn.sigmoid(ag[:, cw:])

    first = CONV_HALO - (CONV_KERNEL - 1)
    span = tm + CONV_HALO - SUBLANES
    for p in range(1, SUBLANES):
        sh_ref[p - 1, 0:span, :] = buf_ref[p:p + span, :]

    @pl.when(i == 0)
    def _():
        for j in range(CONV_KERNEL):
            wtab_ref[j] = jnp.broadcast_to(cw_ref[j:j + 1, :], wtab_ref.shape[1:])

    bias = cb_ref[...]
    g = cg_ref[...]
    beta = cbeta_ref[...]
    for r0 in range(0, tm, rb):
        acc = jnp.broadcast_to(bias, (rb, bias.shape[1]))
        for j in range(CONV_KERNEL):
            p = (first + j) % SUBLANES
            a0 = r0 + first + j - p
            window = buf_ref[a0:a0 + rb, :] if p == 0 else sh_ref[p - 1, a0:a0 + rb, :]
            acc = acc + jnp.concatenate([wtab_ref[j]] * (rb // SUBLANES), axis=0) * window
        y = _layernorm_rows(acc, g, beta)
        u_ref[r0:r0 + rb, :] = (y * jax.nn.sigmoid(y)).astype(u_ref.dtype)

    cos = cos_ref[...]
    sin = sin_ref[...]
    lane = lax.broadcasted_iota(jnp.int32, cos.shape, 1)
    first_half = (lane % HEAD_DIM) < (HEAD_DIM // 2)

    def rope(z, scale):
        outs = []
        for c in range(0, qk, LANES):
            zz = z[:, c:c + LANES]
            swapped = jnp.where(first_half, pltpu.roll(zz, LANES - HEAD_DIM // 2, 1),
                                pltpu.roll(zz, HEAD_DIM // 2, 1))
            outs.append((zz * cos + swapped * sin) * scale)
        return jnp.concatenate(outs, axis=1)

    q_ref[...] = rope(proj(2 * cw, qk), HEAD_DIM ** -0.5 * math.log2(math.e)).astype(BF16)
    k_ref[...] = rope(proj(2 * cw + qk, qk), 1.0).astype(BF16)
    v_ref[...] = proj(2 * cw + 2 * qk, v_ref.shape[1]).astype(BF16)


def _inproj(x2, w_bf, b, cos_t, sin_t, conv_w, conv_b, conv_g, conv_beta, *, seq, tm, rb=32):
    t, d = x2.shape
    n = w_bf.shape[1]
    cw = d // 2
    qk = 2 * HEADS * HEAD_DIM
    vw = HEADS * VALUE_DIM
    tiles_per_seq = seq // tm
    const = lambda i: (0, 0)
    return pl.pallas_call(
        functools.partial(_inproj_kernel, cw=cw, qk=qk, tiles_per_seq=tiles_per_seq, rb=rb),
        grid=(t // tm,),
        in_specs=[
            pl.BlockSpec((tm, d), lambda i: (i, 0)),
            pl.BlockSpec((d, n), const),
            pl.BlockSpec((1, n), const),
            pl.BlockSpec((tm, LANES), lambda i: (i % tiles_per_seq, 0)),
            pl.BlockSpec((tm, LANES), lambda i: (i % tiles_per_seq, 0)),
            pl.BlockSpec((CONV_KERNEL, cw), const),
            pl.BlockSpec((1, cw), const),
            pl.BlockSpec((1, cw), const),
            pl.BlockSpec((1, cw), const),
        ],
        out_specs=[
            pl.BlockSpec((tm, cw), lambda i: (i, 0)),
            pl.BlockSpec((tm, qk), lambda i: (i, 0)),
            pl.BlockSpec((tm, qk), lambda i: (i, 0)),
            pl.BlockSpec((tm, vw), lambda i: (i, 0)),
        ],
        out_shape=[
            jax.ShapeDtypeStruct((t, cw), BF16),
            jax.ShapeDtypeStruct((t, qk), BF16),
            jax.ShapeDtypeStruct((t, qk), BF16),
            jax.ShapeDtypeStruct((t, vw), BF16),
        ],
        scratch_shapes=[pltpu.VMEM((CONV_HALO + tm, cw), F32), pltpu.VMEM((SUBLANES - 1, CONV_HALO + tm, cw), F32),
                        pltpu.VMEM((CONV_KERNEL, SUBLANES, cw), F32)],
        compiler_params=pltpu.CompilerParams(dimension_semantics=("arbitrary",), vmem_limit_bytes=VMEM_LIMIT),
        name="inproj_conv",
    )(x2, w_bf, b, cos_t, sin_t, conv_w, conv_b, conv_g, conv_beta)


def _attn_kernel(lam_ref, g_ref, q_ref, k_ref, v_ref, o_ref, qq_scr, s_scr, p_scr, a_scr, m_scr, l_scr, acc_scr,
                 *, tq, rc, lam_init):
    qi = pl.program_id(2)
    q = q_ref[...]
    lane = lax.broadcasted_iota(jnp.int32, q.shape, 1)
    zero = jnp.zeros_like(q)
    qq_scr[0:tq, :] = jnp.where(lane < HEAD_DIM, q, zero)
    qq_scr[tq:, :] = jnp.where(lane >= HEAD_DIM, q, zero)

    m_scr[...] = jnp.full(m_scr.shape, -jnp.inf, F32)
    l_scr[...] = jnp.zeros(l_scr.shape, F32)
    acc_scr[...] = jnp.zeros(acc_scr.shape, F32)

    def kv_rows(j):
        return pl.ds(pl.multiple_of(j * tq, tq), tq)

    def scores(j, slot):
        s_scr[slot] = lax.dot_general(qq_scr[...], k_ref[kv_rows(j), :], (((1,), (1,)), ((), ())),
                                      preferred_element_type=F32)

    def softmax(slot, masked):
        for r0 in range(0, 2 * tq, rc):
            s = s_scr[slot, r0:r0 + rc, :]
            if masked:
                row = (lax.broadcasted_iota(jnp.int32, s.shape, 0) + r0) % tq
                col = lax.broadcasted_iota(jnp.int32, s.shape, 1)
                s = jnp.where((col // CHUNK) <= (row // CHUNK), s, -jnp.inf)
            m_prev = m_scr[r0:r0 + rc, :]
            m_new = jnp.maximum(m_prev, jnp.max(s, axis=-1, keepdims=True))
            alpha = jnp.exp2(m_prev - m_new)
            p = jnp.exp2(s - jnp.tile(m_new, (1, tq // LANES)))
            l_scr[r0:r0 + rc, :] = alpha * l_scr[r0:r0 + rc, :] + jnp.sum(p, axis=-1, keepdims=True)
            m_scr[r0:r0 + rc, :] = m_new
            a_scr[slot, r0:r0 + rc, :] = alpha
            p_scr[slot, r0:r0 + rc, :] = p.astype(BF16)

    def values(j, slot):
        acc_scr[...] = a_scr[slot] * acc_scr[...] + jnp.dot(p_scr[slot], v_ref[kv_rows(j), :],
                                                            preferred_element_type=F32)

    def step(i, par):
        softmax(par, False)
        values(i - 1, 1 - par)
        scores(i + 1, 1 - par)

    def finish(par):
        softmax(par, True)
        values(qi - 1, 1 - par)
        values(qi, par)

    @pl.when(qi == 0)
    def _():
        scores(0, 0)
        softmax(0, True)
        values(0, 0)

    @pl.when(qi > 0)
    def _():
        scores(0, 0)
        scores(1, 1)
        softmax(0, False)

    def body(i, carry):
        for par in (0, 1):
            @pl.when(i % 2 == par)
            def _():
                step(i, par)
        return carry

    lax.fori_loop(1, qi, body, 0)

    for par in (0, 1):
        @pl.when((qi > 0) & (qi % 2 == par))
        def _():
            finish(par)

    lp = lam_ref[...]
    lam = (jnp.exp(jnp.sum(lp[0:1] * lp[1:2], axis=-1, keepdims=True))
           - jnp.exp(jnp.sum(lp[2:3] * lp[3:4], axis=-1, keepdims=True)) + lam_init)
    o = acc_scr[...] / l_scr[...]
    o = o[:tq] - lam * o[tq:]
    ms = jnp.mean(o * o, axis=-1, keepdims=True)
    o_ref[...] = (o * lax.rsqrt(ms + LN_EPS) * g_ref[...] * (1.0 - lam_init)).astype(o_ref.dtype)


def _diff_attention(q, k, v, lam_params, subln_g, *, batch, seq, tq, lam_init):
    t = q.shape[0]
    nq = seq // tq
    return pl.pallas_call(
        functools.partial(_attn_kernel, tq=tq, rc=min(256, tq), lam_init=lam_init),
        grid=(batch, HEADS, nq),
        in_specs=[
            pl.BlockSpec((4, HEAD_DIM), lambda b, h, i: (0, 0)),
            pl.BlockSpec((1, VALUE_DIM), lambda b, h, i: (0, 0)),
            pl.BlockSpec((tq, VALUE_DIM), lambda b, h, i: (b * nq + i, h)),
            pl.BlockSpec((seq, VALUE_DIM), lambda b, h, i: (b, h)),
            pl.BlockSpec((seq, VALUE_DIM), lambda b, h, i: (b, h)),
        ],
        out_specs=pl.BlockSpec((tq, VALUE_DIM), lambda b, h, i: (b * nq + i, h)),
        out_shape=jax.ShapeDtypeStruct((t, HEADS * VALUE_DIM), BF16),
        scratch_shapes=[
            pltpu.VMEM((2 * tq, VALUE_DIM), BF16),
            pltpu.VMEM((2, 2 * tq, tq), F32),
            pltpu.VMEM((2, 2 * tq, tq), BF16),
            pltpu.VMEM((2, 2 * tq, LANES), F32),
            pltpu.VMEM((2 * tq, LANES), F32),
            pltpu.VMEM((2 * tq, LANES), F32),
            pltpu.VMEM((2 * tq, VALUE_DIM), F32),
        ],
        compiler_params=pltpu.CompilerParams(dimension_semantics=("arbitrary", "arbitrary", "arbitrary"),
                                             vmem_limit_bytes=VMEM_LIMIT),
        name="diff_attention",
    )(lam_params, subln_g, q, k, v)


def _outproj_kernel(x_ref, u_ref, o_ref, w_ref, g_ref, b_ref, wr_ref, br_ref,
                    x1_ref, ri_ref, rw_ref, cnt_ref, carry_scr, *, alpha, cw):
    acc = jnp.dot(u_ref[...], w_ref[0:cw, :], preferred_element_type=F32)
    acc = acc + jnp.dot(o_ref[...], w_ref[cw:, :], preferred_element_type=F32)
    x1 = _layernorm_rows(alpha * x_ref[...] + acc, g_ref[...], b_ref[...])
    x1_ref[...] = x1

    x_hi = x1.astype(BF16)
    x_lo = (x1 - x_hi.astype(F32)).astype(BF16)
    hi = jnp.dot(x_hi, wr_ref[...], preferred_element_type=F32)
    logits = (hi[:, :LANES] + hi[:, LANES:] + jnp.dot(x_lo, wr_ref[:, :LANES], preferred_element_type=F32)
              + br_ref[...])
    lane = lax.broadcasted_iota(jnp.int32, logits.shape, 1)
    neg = -jnp.inf
    gmask = lane < N_GROUPS
    lg = jnp.where(gmask, logits, neg)
    gmax = jnp.max(lg, axis=-1, keepdims=True)
    lane_f = lane.astype(F32)
    first = lambda hit: jnp.min(jnp.where(hit, lane_f, float(LANES)), axis=-1, keepdims=True).astype(jnp.int32)
    grp = first(lg == gmax)
    gsum = jnp.sum(jnp.where(gmask, jnp.exp(lg - gmax), 0.0), axis=-1, keepdims=True)
    gate = 1.0 / gsum
    emask = (lane >= N_GROUPS) & (lane < N_GROUPS + N_EXPERTS) & (((lane - N_GROUPS) // EXPERTS_PER_GROUP) == grp)
    ev = jnp.where(emask, logits, neg)
    v0 = jnp.max(ev, axis=-1, keepdims=True)
    i0 = first(emask & (ev == v0))
    emask1 = emask & (lane != i0)
    ev1 = jnp.where(emask1, logits, neg)
    v1 = jnp.max(ev1, axis=-1, keepdims=True)
    i1 = first(emask1 & (ev1 == v1))
    e1 = jnp.exp(v1 - v0)
    w0 = gate * (1.0 / (1.0 + e1))
    w1 = gate * (e1 / (1.0 + e1))

    @pl.when(pl.program_id(0) == 0)
    def _():
        carry_scr[...] = jnp.zeros(carry_scr.shape, F32)

    tm = logits.shape[0]
    oh0 = (lane == i0 - N_GROUPS).astype(F32)
    oh1 = (lane == i1 - N_GROUPS).astype(F32)
    both = oh0 + oh1
    tri = (lax.broadcasted_iota(jnp.int32, (tm, tm), 0) > lax.broadcasted_iota(jnp.int32, (tm, tm), 1)).astype(BF16)
    before = jnp.dot(tri, both.astype(BF16), preferred_element_type=F32) + carry_scr[...]
    rank0 = jnp.sum(oh0 * before, axis=-1, keepdims=True)
    rank1 = jnp.sum(oh1 * before, axis=-1, keepdims=True)
    carry_scr[...] = carry_scr[...] + jnp.sum(both, axis=0, keepdims=True)
    cnt_ref[...] = carry_scr[...]

    choice = jnp.where(lane == 0, (i0 - N_GROUPS).astype(F32), jnp.where(lane == 1, (i1 - N_GROUPS).astype(F32),
                       jnp.where(lane == 2, rank0, jnp.where(lane == 3, rank1, 0.0))))
    ri_ref[...] = jnp.transpose(choice)[0:SUBLANES, :]
    rw_ref[...] = jnp.where(lane == 0, w0, jnp.where(lane == 1, w1, 0.0))


def _outproj_router(x2, u2, o, w_bf, g, b, wr, br, *, tm, alpha):
    t, d = x2.shape
    cw = u2.shape[1]
    row = lambda i: (i, 0)
    const = lambda i: (0, 0)
    return pl.pallas_call(
        functools.partial(_outproj_kernel, alpha=alpha, cw=cw),
        grid=(t // tm,),
        in_specs=[
            pl.BlockSpec((tm, d), row),
            pl.BlockSpec((tm, cw), row),
            pl.BlockSpec((tm, o.shape[1]), row),
            pl.BlockSpec(w_bf.shape, const),
            pl.BlockSpec((1, d), const),
            pl.BlockSpec((1, d), const),
            pl.BlockSpec((d, 2 * LANES), const),
            pl.BlockSpec((1, LANES), const),
        ],
        out_specs=[
            pl.BlockSpec((tm, d), row),
            pl.BlockSpec((SUBLANES, tm), lambda i: (0, i)),
            pl.BlockSpec((tm, LANES), row),
            pl.BlockSpec((1, LANES), const),
        ],
        out_shape=[
            jax.ShapeDtypeStruct((t, d), F32),
            jax.ShapeDtypeStruct((SUBLANES, t), F32),
            jax.ShapeDtypeStruct((t, LANES), F32),
            jax.ShapeDtypeStruct((1, LANES), F32),
        ],
        scratch_shapes=[pltpu.VMEM((1, LANES), F32)],
        compiler_params=pltpu.CompilerParams(dimension_semantics=("arbitrary",), vmem_limit_bytes=VMEM_LIMIT),
        name="outproj_router",
    )(x2, u2, o, w_bf, g, b, wr, br)


def _dispatch_kernel(pos_ref, pend_ref, x_hbm, xs_hbm, xbuf, pbuf, zbuf, lsem, sem, zsem, *, tm, rp):
    i = pl.program_id(0)
    n = pl.num_programs(0)

    def load(tile, s):
        return pltpu.make_async_copy(x_hbm.at[pl.ds(pl.multiple_of(tile * tm, tm), tm)], xbuf.at[s], lsem.at[s])

    def zero_copy(e):
        start = pl.multiple_of(pend_ref[e + 1] - MOE_BLOCK, MOE_BLOCK)
        return pltpu.make_async_copy(zbuf, xs_hbm.at[pl.ds(start, MOE_BLOCK)], zsem)

    def zero_block(b):
        return pltpu.make_async_copy(zbuf, xs_hbm.at[pl.ds(pl.multiple_of(b * MOE_BLOCK, MOE_BLOCK), MOE_BLOCK)], zsem)

    @pl.when(i == 0)
    def _():
        load(0, 0).start()
        zbuf[...] = jnp.zeros(zbuf.shape, zbuf.dtype)
        first_unused = pend_ref[N_EXPERTS] // MOE_BLOCK
        n_blocks = xs_hbm.shape[0] // MOE_BLOCK
        for e in range(N_EXPERTS):
            @pl.when(pend_ref[e + 1] > pend_ref[e])
            def _():
                zero_copy(e).start()

        def start_unused(b, c):
            zero_block(b).start()
            return c

        def wait_unused(b, c):
            zero_block(b).wait()
            return c

        lax.fori_loop(first_unused, n_blocks, start_unused, 0)
        for e in range(N_EXPERTS):
            @pl.when(pend_ref[e + 1] > pend_ref[e])
            def _():
                zero_copy(e).wait()
        lax.fori_loop(first_unused, n_blocks, wait_unused, 0)

    def row_copy(s, r, dst_row):
        return pltpu.make_async_copy(pbuf.at[s, pl.ds(r % tm, 1)], xs_hbm.at[pl.ds(dst_row, 1)], sem.at[s])

    def drain(s):
        for r in range(2 * tm):
            row_copy(s, r, 0).wait()

    for slot in (0, 1):
        @pl.when(i % 2 == slot)
        def _():
            load(i, slot).wait()

            @pl.when(i > 0)
            def _():
                drain(1 - slot)

            @pl.when(i + 1 < n)
            def _():
                load(i + 1, 1 - slot).start()

            for r0 in range(0, tm, rp):
                pbuf[slot, r0:r0 + rp, :] = _pack_bf16_pairs(xbuf[slot, r0:r0 + rp, :])

            for r in range(2 * tm):
                row_copy(slot, r, pos_ref[i * 2 * tm + r]).start(priority=r % 2)

            @pl.when(i == n - 1)
            def _():
                drain(slot)


def _dispatch(pos, pad_ends, x1, *, n_rows, tm):
    t, d = x1.shape
    grid_spec = pltpu.PrefetchScalarGridSpec(
        num_scalar_prefetch=2,
        grid=(t // tm,),
        in_specs=[pl.BlockSpec(memory_space=pl.ANY)],
        out_specs=pl.BlockSpec(memory_space=pl.ANY),
        scratch_shapes=[pltpu.VMEM((2, tm, d), x1.dtype), pltpu.VMEM((2, tm, d // 2), jnp.uint32),
                        pltpu.VMEM((MOE_BLOCK, d // 2), jnp.uint32),
                        pltpu.SemaphoreType.DMA((2,)), pltpu.SemaphoreType.DMA((2,)), pltpu.SemaphoreType.DMA(())],
    )
    return pl.pallas_call(
        functools.partial(_dispatch_kernel, tm=tm, rp=min(128, tm)),
        grid_spec=grid_spec,
        out_shape=jax.ShapeDtypeStruct((n_rows, d // 2), jnp.uint32),
        compiler_params=pltpu.CompilerParams(dimension_semantics=("arbitrary",)),
        name="moe_dispatch",
    )(pos, pad_ends, x1)


def _expert_kernel(blk_e_ref, nused_ref, xs_ref, wg_hbm, wu_hbm, wd_hbm, y_ref,
                   wg_f32, wu_f32, wd_f32, wg_bf, wu_bf, wd_bf, sem, slot_scr, *, layer):
    i = pl.program_id(0)
    n_used = nused_ref[0]
    last_block = pl.num_programs(0) - 1

    def fetch(e, s):
        return [pltpu.make_async_copy(src.at[layer, e], dst.at[s], sem.at[s])
                for src, dst in ((wg_hbm, wg_f32), (wu_hbm, wu_f32), (wd_hbm, wd_f32))]

    @pl.when(i < n_used)
    def _():
        @pl.when((i == 0) | (blk_e_ref[i] != blk_e_ref[jnp.maximum(i - 1, 0)]))
        def _():
            e = blk_e_ref[i]

            @pl.when(i == 0)
            def _():
                slot_scr[0] = 0
                for c in fetch(e, 0):
                    c.start()

            s = slot_scr[0]
            for c in fetch(e, s):
                c.wait()

            nxt = lax.while_loop(lambda j: (j < n_used) & (blk_e_ref[jnp.minimum(j, last_block)] == e),
                                 lambda j: j + 1, i + 1)

            @pl.when(nxt < n_used)
            def _():
                for c in fetch(blk_e_ref[jnp.minimum(nxt, last_block)], 1 - s):
                    c.start()

            slot_scr[0] = 1 - s
            wg_bf[...] = wg_f32[s].astype(BF16)
            wu_bf[...] = wu_f32[s].astype(BF16)
            wd_bf[...] = wd_f32[s].astype(BF16)

        xb = _unpack_bf16_pairs(xs_ref[...])
        gate = jnp.dot(xb, wg_bf[...], preferred_element_type=F32)
        up = jnp.dot(xb, wu_bf[...], preferred_element_type=F32)
        h = (gate * jax.nn.sigmoid(gate)) * up
        y_ref[...] = jnp.dot(h.astype(BF16), wd_bf[...], preferred_element_type=F32)

    @pl.when(i >= nused_ref[0])
    def _():
        y_ref[...] = jnp.zeros(y_ref.shape, y_ref.dtype)


def _expert_mlp(blk_e, n_used, xs, wg, wu, wd, *, layer):
    n_rows = xs.shape[0]
    d = wg.shape[2]
    n_blocks = n_rows // MOE_BLOCK
    de = wg.shape[3]
    live = lambda i, be, nu: jnp.maximum(jnp.minimum(i, nu[0] - 1), 0)
    hbm = pl.BlockSpec(memory_space=pl.ANY)
    grid_spec = pltpu.PrefetchScalarGridSpec(
        num_scalar_prefetch=2,
        grid=(n_blocks,),
        in_specs=[pl.BlockSpec((MOE_BLOCK, xs.shape[1]), lambda i, be, nu: (live(i, be, nu), 0)), hbm, hbm, hbm],
        out_specs=pl.BlockSpec((MOE_BLOCK, d), lambda i, be, nu: (i, 0)),
        scratch_shapes=[
            pltpu.VMEM((2, d, de), F32), pltpu.VMEM((2, d, de), F32), pltpu.VMEM((2, de, d), F32),
            pltpu.VMEM((d, de), BF16), pltpu.VMEM((d, de), BF16), pltpu.VMEM((de, d), BF16),
            pltpu.SemaphoreType.DMA((2,)),
            pltpu.SMEM((1,), jnp.int32),
        ],
    )
    return pl.pallas_call(
        functools.partial(_expert_kernel, layer=layer),
        grid_spec=grid_spec,
        out_shape=jax.ShapeDtypeStruct((n_rows, d), F32),
        compiler_params=pltpu.CompilerParams(dimension_semantics=("arbitrary",), vmem_limit_bytes=VMEM_LIMIT),
        name="expert_mlp",
    )(blk_e, n_used, xs, wg, wu, wd)


def _row_gather_copy(src_hbm, dst_buf, sem, slot, row, src_row):
    return pltpu.make_async_copy(src_hbm.at[pl.ds(src_row, 1)], dst_buf.at[slot, pl.ds(row, 1)], sem.at[slot])


def _combine_kernel(pos_ref, y_hbm, x_ref, rw_ref, g_ref, b_ref, o_ref, ybuf, sem, *, tm, alpha):
    i = pl.program_id(0)
    n = pl.num_programs(0)

    def issue(tile, s):
        for r in range(2 * tm):
            _row_gather_copy(y_hbm, ybuf, sem, s, r, pos_ref[tile * 2 * tm + r]).start(priority=r % 2)

    def drain(s):
        for r in range(2 * tm):
            _row_gather_copy(y_hbm, ybuf, sem, s, r, 0).wait()

    @pl.when(i == 0)
    def _():
        issue(0, 0)
        issue(jnp.minimum(1, n - 1), 1)

    for slot in (0, 1):
        @pl.when(i % 2 == slot)
        def _():
            drain(slot)
            yy = ybuf[slot]
            rw = rw_ref[...]
            f = yy[:tm] * rw[:, 0:1] + yy[tm:] * rw[:, 1:2]
            o_ref[...] = _layernorm_rows(alpha * x_ref[...] + f, g_ref[...], b_ref[...])
            issue(jnp.minimum(i + 2, n - 1), slot)

            @pl.when(i == n - 1)
            def _():
                drain(slot)
                drain(1 - slot)


def _combine(pos, y, x1, route_w, g, b, *, tm, alpha):
    t, d = x1.shape
    grid_spec = pltpu.PrefetchScalarGridSpec(
        num_scalar_prefetch=1,
        grid=(t // tm,),
        in_specs=[
            pl.BlockSpec(memory_space=pl.ANY),
            pl.BlockSpec((tm, d), lambda i, pos: (i, 0)),
            pl.BlockSpec((tm, LANES), lambda i, pos: (i, 0)),
            pl.BlockSpec((1, d), lambda i, pos: (0, 0)),
            pl.BlockSpec((1, d), lambda i, pos: (0, 0)),
        ],
        out_specs=pl.BlockSpec((tm, d), lambda i, pos: (i, 0)),
        scratch_shapes=[pltpu.VMEM((2, 2 * tm, d), F32), pltpu.SemaphoreType.DMA((2,))],
    )
    return pl.pallas_call(
        functools.partial(_combine_kernel, tm=tm, alpha=alpha),
        grid_spec=grid_spec,
        out_shape=jax.ShapeDtypeStruct((t, d), F32),
        compiler_params=pltpu.CompilerParams(dimension_semantics=("arbitrary",), vmem_limit_bytes=VMEM_LIMIT),
        name="moe_combine",
    )(pos, y, x1, route_w, g, b)


def _tile_major(dest, tm):
    return dest.reshape(2, dest.shape[1] // tm, tm).transpose(1, 0, 2).reshape(-1)


def _dispatch_plan(route_i, counts):
    t = route_i.shape[1]
    counts = counts[0, :N_EXPERTS].astype(jnp.int32)
    padded = ((counts + MOE_BLOCK - 1) // MOE_BLOCK) * MOE_BLOCK
    pad_end = jnp.cumsum(padded)
    pad_start = pad_end - padded
    choice = route_i[0:4].astype(jnp.int32)
    dest = choice[2:4]
    for e in range(N_EXPERTS):
        dest = dest + jnp.where(choice[0:2] == e, pad_start[e], 0)
    n_blocks = -(-(2 * t) // MOE_BLOCK) + N_EXPERTS
    block_start = jnp.arange(n_blocks, dtype=jnp.int32) * MOE_BLOCK
    blk_e = jnp.minimum(jnp.sum((pad_end[None, :] <= block_start[:, None]).astype(jnp.int32), axis=1), N_EXPERTS - 1)
    pad_ends = jnp.concatenate([jnp.zeros((1,), jnp.int32), pad_end.astype(jnp.int32)])
    n_used = (pad_end[-1:] // MOE_BLOCK).astype(jnp.int32)
    return dest.astype(jnp.int32), pad_ends, blk_e, n_used, n_blocks * MOE_BLOCK


def _rope_tables(seq):
    half = HEAD_DIM // 2
    inv_freq = 1.0 / (ROPE_THETA ** (jnp.arange(half, dtype=F32) * 2.0 / HEAD_DIM))
    ang = jnp.arange(seq, dtype=F32)[:, None] * inv_freq[None, :]
    cos, sin = jnp.cos(ang), jnp.sin(ang)
    reps = LANES // HEAD_DIM
    return jnp.tile(jnp.concatenate([cos, cos], axis=1), (1, reps)), jnp.tile(jnp.concatenate([-sin, sin], axis=1), (1, reps))


def _pick_tile(n, want):
    while n % want:
        want //= 2
    return want


def kernel(x, w_in, b_in, conv_w, conv_b, conv_ln_g, conv_ln_b, lam_q1, lam_k1, lam_q2, lam_k2, subln_g, w_out, ln1_g, ln1_b, w_rg, b_rg, w_re, b_re, w_gate_e, w_up_e, w_down_e, ln2_g, ln2_b):
    batch, seq, d = x.shape
    depth = w_in.shape[0]
    t = batch * seq
    alpha = (2.0 * depth) ** 0.25
    cos_t, sin_t = _rope_tables(seq)
    tm = _pick_tile(seq, PROJ_ROWS)
    tq = _pick_tile(seq, ATTN_ROWS)
    tc = _pick_tile(seq, COMBINE_ROWS)
    td = _pick_tile(seq, DISPATCH_ROWS)
    row = lambda a: a.reshape(1, -1)

    x2 = x.reshape(t, d)
    for l in range(depth):
        lam_init = 0.8 - 0.6 * math.exp(-0.3 * l)
        u2, q, k, v = _inproj(x2, w_in[l].astype(BF16), row(b_in[l]), cos_t, sin_t,
                              conv_w[l].reshape(CONV_KERNEL, -1), row(conv_b[l]), row(conv_ln_g[l]),
                              row(conv_ln_b[l]), seq=seq, tm=tm)
        lam_params = jnp.stack([lam_q1[l], lam_k1[l], lam_q2[l], lam_k2[l]])
        o = _diff_attention(q, k, v, lam_params, row(subln_g[l]), batch=batch, seq=seq, tq=tq, lam_init=lam_init)
        w_router = jnp.concatenate([w_rg[l], w_re[l]], axis=1)
        w_router = jnp.pad(w_router, ((0, 0), (0, LANES - w_router.shape[1])))
        w_router_hi = w_router.astype(BF16)
        w_router = jnp.concatenate([w_router_hi, (w_router - w_router_hi.astype(F32)).astype(BF16)], axis=1)
        b_router = jnp.pad(jnp.concatenate([b_rg[l], b_re[l]]), (0, LANES - N_GROUPS - N_EXPERTS))
        x1, route_i, route_w, counts = _outproj_router(x2, u2, o, w_out[l].astype(BF16), row(ln1_g[l]), row(ln1_b[l]),
                                                       w_router, row(b_router), tm=tm, alpha=alpha)
        dest, pad_ends, blk_e, n_used, n_rows = _dispatch_plan(route_i, counts)
        xs = _dispatch(_tile_major(dest, td), pad_ends, x1, n_rows=n_rows, tm=td)
        y = _expert_mlp(blk_e, n_used, xs, w_gate_e, w_up_e, w_down_e, layer=l)
        x2 = _combine(_tile_major(dest, tc), y, x1, route_w, row(ln2_g[l]), row(ln2_b[l]), tm=tc, alpha=alpha)
    return x2.reshape(batch, seq, d)
```
